```python
import jax, jax.numpy as jnp
from jax import lax
import numpy as np

D_MODEL = 1024
BATCH = 8
SEQ = 2048
DEPTH = 2
DEC_BATCH = 128
DEC_SEQ = 4
PAST_LEN = 2048
PAGE_SIZE = 128

PLE_DIM = 256
N_HEADS_SB = 8
HEAD_DIM_SB = 64
D_SB = N_HEADS_SB * HEAD_DIM_SB
N_GROUPS_MLP = 8
GROUP_DIM_MLP = 64
D_MLP = N_GROUPS_MLP * GROUP_DIM_MLP
D_MIX = D_SB + D_MLP
D_IN = 3 * D_SB + 2 * D_MLP
CHUNK = 128
Q_BLOCK = 128
D_FF = 2816
CONV_W = 3
EPS = 1e-6
SB_BIAS_INIT = -7.0

kernel_name = "hymba_stickbreak_chunkmlp_convffn_ple_step"


def rmsnorm(x, g):
    xf = x.astype(jnp.float32)
    xf = xf * lax.rsqrt(jnp.mean(xf * xf, axis=-1, keepdims=True) + EPS)
    return (xf * g.astype(jnp.float32)).astype(x.dtype)


def split_projection(hn, w_in):
    z = hn @ w_in
    q, k, v, u, vm = jnp.split(z, [D_SB, 2 * D_SB, 3 * D_SB, 3 * D_SB + D_MLP], axis=-1)
    heads = lambda t: t.reshape(t.shape[:-1] + (N_HEADS_SB, HEAD_DIM_SB))
    return heads(q), heads(k), heads(v), jax.nn.gelu(u), jax.nn.gelu(vm)


def stick_breaking(q, k, v, sb_bias, q_pos, k_pos):
    z = jnp.einsum('bqhd,bkhd->bhqk', q.astype(jnp.float32), k.astype(jnp.float32)) * (HEAD_DIM_SB ** -0.5)
    z = z + sb_bias.astype(jnp.float32)[None, :, None, None]
    mask = k_pos[None, :] < q_pos[:, None]
    log_1mb = jnp.where(mask, jax.nn.log_sigmoid(-z), 0.0)
    suffix = lax.cumsum(log_1mb, axis=3, reverse=True) - log_1mb
    a = jnp.where(mask, jnp.exp(jax.nn.log_sigmoid(z) + suffix), 0.0)
    o = jnp.einsum('bhqk,bkhd->bqhd', a, v.astype(jnp.float32))
    return o.astype(v.dtype)


def sb_prompt(q, k, v, sb_bias):
    b, s, h, d = q.shape
    nb = s // Q_BLOCK
    qb = jnp.moveaxis(q.reshape(b, nb, Q_BLOCK, h, d), 1, 0)
    k_pos = jnp.arange(s, dtype=jnp.int32)

    def block(args):
        q_blk, i = args
        q_pos = i * Q_BLOCK + jnp.arange(Q_BLOCK, dtype=jnp.int32)
        return stick_breaking(q_blk, k, v, sb_bias, q_pos, k_pos)

    ob = lax.map(block, (qb, jnp.arange(nb, dtype=jnp.int32)))
    return jnp.moveaxis(ob, 0, 1).reshape(b, s, h * d)


def causal_spatial_weights(w_s):
    return w_s * jnp.tril(jnp.ones((CHUNK, CHUNK), w_s.dtype))


def chunk_mlp_prompt(u, vm, w_s, b_s):
    b, s, _ = vm.shape
    vc = vm.reshape(b, s // CHUNK, CHUNK, N_GROUPS_MLP, GROUP_DIM_MLP)
    mixed = jnp.einsum('gts,bcsgd->bctgd', causal_spatial_weights(w_s), vc)
    mixed = mixed + jnp.swapaxes(b_s, 0, 1)[:, :, None]
    return u * mixed.reshape(b, s, D_MLP)


def chunk_mlp_sample(u, vm, w_s, b_s):
    b, t, _ = vm.shape
    vc = vm.reshape(b, t, N_GROUPS_MLP, GROUP_DIM_MLP)
    w = causal_spatial_weights(w_s)[:, :t, :t]
    mixed = jnp.einsum('gts,bsgd->btgd', w, vc) + jnp.swapaxes(b_s[:, :t], 0, 1)[:, :, None]
    return u * mixed.reshape(b, t, D_MLP)


def mix_merge(o_sb, o_mlp, gn_sb, gn_mlp, w_out):
    return jnp.concatenate([rmsnorm(o_sb, gn_sb), rmsnorm(o_mlp, gn_mlp)], axis=-1) @ w_out


def conv_ffn(hn, prefix, w_up, conv_w, conv_b, w_down):
    gate, up = jnp.split(hn @ w_up, 2, axis=-1)
    t = gate.shape[1]
    gp = jnp.concatenate([prefix.astype(gate.dtype), gate], axis=1)
    conv = conv_b
    for j in range(CONV_W):
        conv = conv + conv_w[j] * gp[:, j:j + t]
    y = (jax.nn.gelu(conv) * up) @ w_down
    return y, gp[:, gp.shape[1] - (CONV_W - 1):]


def per_layer_embedding(h, p_i, g, w_gate, w_proj):
    gate = jax.nn.sigmoid(rmsnorm(h, g) @ w_gate)
    return h + gate * (p_i @ w_proj)


def setup_inputs(seed: int = 0) -> dict:
    key = jax.random.key(seed)
    ks = jax.random.split(key, 28)
    f32 = jnp.float32
    nrm = lambda k, shape, scale: jax.random.normal(k, shape, f32) * scale
    n_pages = PAST_LEN // PAGE_SIZE
    n_used = DEC_BATCH * n_pages
    n_phys = n_used + n_used // 4
    page_table = jax.random.permutation(ks[0], n_phys)[:n_used].reshape(DEC_BATCH, n_pages).astype(jnp.int32)
    return {
        "x_prompt": nrm(ks[1], (BATCH, SEQ, D_MODEL), 1.0),
        "x_sample": nrm(ks[2], (DEC_BATCH, DEC_SEQ, D_MODEL), 1.0),
        "cache_k": nrm(ks[3], (DEPTH, n_phys, PAGE_SIZE, N_HEADS_SB, HEAD_DIM_SB), 1.0),
        "cache_v": nrm(ks[4], (DEPTH, n_phys, PAGE_SIZE, N_HEADS_SB, HEAD_DIM_SB), 1.0),
        "state_conv": nrm(ks[5], (DEPTH, DEC_BATCH, CONV_W - 1, D_FF), 1.0),
        "page_table": page_table,
        "p_prompt": nrm(ks[6], (DEPTH, BATCH, SEQ, PLE_DIM), 1.0),
        "p_sample": nrm(ks[7], (DEPTH, DEC_BATCH, DEC_SEQ, PLE_DIM), 1.0),
        "norm_mix_g": 1.0 + nrm(ks[8], (DEPTH, D_MODEL), 0.1),
        "w_in": nrm(ks[9], (DEPTH, D_MODEL, D_IN), D_MODEL ** -0.5),
        "sb_bias": SB_BIAS_INIT + nrm(ks[24], (DEPTH, N_HEADS_SB), 0.5),
        "gn_sb": 1.0 + nrm(ks[10], (DEPTH, D_SB), 0.1),
        "gn_mlp": 1.0 + nrm(ks[11], (DEPTH, D_MLP), 0.1),
        "w_spatial": nrm(ks[12], (DEPTH, N_GROUPS_MLP, CHUNK, CHUNK), CHUNK ** -0.5),
        "b_spatial": 1.0 + nrm(ks[13], (DEPTH, N_GROUPS_MLP, CHUNK), 0.1),
        "w_out": nrm(ks[14], (DEPTH, D_MIX, D_MODEL), D_MIX ** -0.5),
        "norm_ffn_g": 1.0 + nrm(ks[15], (DEPTH, D_MODEL), 0.1),
        "w_up": nrm(ks[16], (DEPTH, D_MODEL, 2 * D_FF), D_MODEL ** -0.5),
        "conv_w": nrm(ks[17], (DEPTH, CONV_W, D_FF), CONV_W ** -0.5),
        "conv_b": nrm(ks[18], (DEPTH, D_FF), 0.02),
        "w_down": nrm(ks[19], (DEPTH, D_FF, D_MODEL), D_FF ** -0.5),
        "norm_ple_g": 1.0 + nrm(ks[20], (DEPTH, D_MODEL), 0.1),
        "w_ple_gate": nrm(ks[21], (DEPTH, D_MODEL, D_MODEL), D_MODEL ** -0.5),
        "w_ple_proj": nrm(ks[22], (DEPTH, PLE_DIM, D_MODEL), PLE_DIM ** -0.5),
        "final_norm_g": 1.0 + nrm(ks[23], (D_MODEL,), 0.1),
    }


def reference(x_prompt, x_sample, cache_k, cache_v, state_conv, page_table, p_prompt, p_sample,
              norm_mix_g, w_in, sb_bias, gn_sb, gn_mlp, w_spatial, b_spatial, w_out,
              norm_ffn_g, w_up, conv_w, conv_b, w_down,
              norm_ple_g, w_ple_gate, w_ple_proj, final_norm_g):
    n_pages = page_table.shape[1]
    past = n_pages * PAGE_SIZE
    bp = x_prompt.shape[0]
    dec_b, dec_t = x_sample.shape[0], x_sample.shape[1]
    q_pos_s = past + jnp.arange(dec_t, dtype=jnp.int32)
    k_pos_s = jnp.arange(past + dec_t, dtype=jnp.int32)
    hp, hs = x_prompt, x_sample
    kp_l, vp_l, ks_l, vs_l, cp_l, cs_l, ms_l = [], [], [], [], [], [], []
    for i in range(DEPTH):
        q, k, v, u, vm = split_projection(rmsnorm(hp, norm_mix_g[i]), w_in[i])
        o_sb = sb_prompt(q, k, v, sb_bias[i])
        o_mlp = chunk_mlp_prompt(u, vm, w_spatial[i], b_spatial[i])
        hp = hp + mix_merge(o_sb, o_mlp, gn_sb[i], gn_mlp[i], w_out[i])
        zero_prefix = jnp.zeros((bp, CONV_W - 1, D_FF), hp.dtype)
        y, conv_p = conv_ffn(rmsnorm(hp, norm_ffn_g[i]), zero_prefix, w_up[i], conv_w[i], conv_b[i], w_down[i])
        hp = hp + y
        hp = per_layer_embedding(hp, p_prompt[i], norm_ple_g[i], w_ple_gate[i], w_ple_proj[i])
        kp_l.append(k); vp_l.append(v); cp_l.append(conv_p)

        q, k, v, u, vm = split_projection(rmsnorm(hs, norm_mix_g[i]), w_in[i])
        k_past = cache_k[i][page_table].reshape(dec_b, past, N_HEADS_SB, HEAD_DIM_SB)
        v_past = cache_v[i][page_table].reshape(dec_b, past, N_HEADS_SB, HEAD_DIM_SB)
        k_all = jnp.concatenate([k_past, k.astype(k_past.dtype)], axis=1)
        v_all = jnp.concatenate([v_past, v.astype(v_past.dtype)], axis=1)
        o_sb = stick_breaking(q, k_all, v_all, sb_bias[i], q_pos_s, k_pos_s).reshape(dec_b, dec_t, D_SB)
        o_mlp = chunk_mlp_sample(u, vm, w_spatial[i], b_spatial[i])
        hs = hs + mix_merge(o_sb.astype(hs.dtype), o_mlp, gn_sb[i], gn_mlp[i], w_out[i])
        y, conv_s = conv_ffn(rmsnorm(hs, norm_ffn_g[i]), state_conv[i], w_up[i], conv_w[i], conv_b[i], w_down[i])
        hs = hs + y
        hs = per_layer_embedding(hs, p_sample[i], norm_ple_g[i], w_ple_gate[i], w_ple_proj[i])
        ks_l.append(k); vs_l.append(v); cs_l.append(conv_s); ms_l.append(vm)

    y_prompt = rmsnorm(hp, final_norm_g)
    y_sample = rmsnorm(hs, final_norm_g)
    k_prompt = jnp.stack(kp_l)
    v_prompt = jnp.stack(vp_l)
    k_sample = jnp.stack(ks_l)
    v_sample = jnp.stack(vs_l)
    conv_prompt = jnp.stack(cp_l)
    conv_sample = jnp.stack(cs_l)
    mlp_v_sample = jnp.stack(ms_l)
    return (y_prompt, y_sample, k_prompt, v_prompt, k_sample, v_sample, conv_prompt, conv_sample, mlp_v_sample)
```

```python
import functools
import math

import jax
import jax.numpy as jnp
from jax import lax
from jax.experimental import pallas as pl
from jax.experimental.pallas import tpu as pltpu

F32 = jnp.float32
BF16 = jnp.bfloat16

D_MODEL = 1024
N_HEADS = 8
HEAD_DIM = 64
D_SB = N_HEADS * HEAD_DIM
N_GROUPS = 8
GROUP_DIM = 64
D_MLP = N_GROUPS * GROUP_DIM
D_IN = 3 * D_SB + 2 * D_MLP
CHUNK = 128
D_FF = 2816
CONV_W = 3
PLE_DIM = 256
PAGE_SIZE = 128
EPS = 1e-6

LANES = 128
SUBLANES = 8
HEAD_PAIRS = D_SB // LANES
FF_CHUNK = 256
N_FF_CHUNKS = D_FF // FF_CHUNK
TM = 512
TQ = 256
VMEM_LIMIT = 56 * 1024 * 1024


def _bdot(a, b):
    return jnp.dot(a, b, preferred_element_type=F32)


def _rms(x, g):
    ms = jnp.mean(x * x, axis=-1, keepdims=True)
    return x * lax.rsqrt(ms + EPS) * g


def _gelu(x):
    c = math.sqrt(2.0 / math.pi)
    return x * (0.5 * (1.0 + jnp.tanh(c * (x + 0.044715 * (x * x * x)))))


def _resident(shape):
    nd = len(shape)
    return pl.BlockSpec(shape, lambda *_: (0,) * nd, pipeline_mode=pl.Buffered(1))


def _project(h_ref, g_ref, win_ref):
    hn = _rms(h_ref[...], g_ref[...]).astype(BF16)
    seg = lambda j: _bdot(hn, win_ref[:, j * D_SB:(j + 1) * D_SB])
    return seg(0) * (HEAD_DIM ** -0.5), seg(1), seg(2), _gelu(seg(3)), _gelu(seg(4))


def _proj_prompt_kernel(h_ref, g_ref, win_ref, ws_ref, bs_ref, gnm_ref,
                        q_ref, k_ref, v_ref, kb_ref, vb_ref, om_ref):
    q, k, v, u, vm = _project(h_ref, g_ref, win_ref)
    q_ref[...] = q.astype(BF16)
    k_ref[...] = k
    v_ref[...] = v
    kb_ref[...] = k.astype(BF16)
    vb_ref[...] = v.astype(BF16)

    row = lax.broadcasted_iota(jnp.int32, (CHUNK, 2 * CHUNK), 0)
    col = lax.broadcasted_iota(jnp.int32, (CHUNK, 2 * CHUNK), 1)
    tril = (col & (CHUNK - 1)) <= row
    wpair = [jnp.where(tril, ws_ref[p], 0.0).astype(BF16) for p in range(HEAD_PAIRS)]
    lo_half = lax.broadcasted_iota(jnp.int32, (CHUNK, LANES), 1) < GROUP_DIM
    for c in range(h_ref.shape[0] // CHUNK):
        rows = slice(c * CHUNK, (c + 1) * CHUNK)
        parts = []
        for p in range(HEAD_PAIRS):
            s = vm[rows, p * LANES:(p + 1) * LANES]
            rhs = jnp.concatenate([jnp.where(lo_half, s, 0.0).astype(BF16),
                                   jnp.where(lo_half, 0.0, s).astype(BF16)], axis=0)
            parts.append(_bdot(wpair[p], rhs))
        mixed = jnp.concatenate(parts, axis=1) + bs_ref[...]
        om_ref[rows, :] = _rms(u[rows] * mixed, gnm_ref[...]).astype(BF16)


def _proj_sample_kernel(h_ref, g_ref, win_ref, wrow_ref, brow_ref, gnm_ref,
                        q_ref, k_ref, v_ref, vm_ref, om_ref, *, n_tok, n_seq):
    q, k, v, u, vm = _project(h_ref, g_ref, win_ref)
    q_ref[...] = q
    k_ref[...] = k
    v_ref[...] = v
    vm_ref[...] = vm
    for t in range(n_tok):
        mixed = brow_ref[t:t + 1, :]
        for s in range(t + 1):
            w = wrow_ref[t * n_tok + s:t * n_tok + s + 1, :]
            mixed = mixed + w * vm[s * n_seq:(s + 1) * n_seq]
        rows = slice(t * n_seq, (t + 1) * n_seq)
        om_ref[rows, :] = _rms(u[rows] * mixed, gnm_ref[...]).astype(BF16)


def _sb_block(qm, kblk, vblk, bias, carry, ucat, mask):
    s = lax.dot_general(qm, kblk, (((1,), (1,)), ((), ())), preferred_element_type=F32)
    z = s + bias
    lg = jnp.log(1.0 + jnp.exp(-jnp.abs(z)))
    log_b = jnp.minimum(z, 0.0) - lg
    log_1mb = log_b - z
    if mask is not None:
        log_1mb = jnp.where(mask, log_1mb, 0.0)
    hi = log_1mb.astype(BF16)
    lo = (log_1mb - hi.astype(F32)).astype(BF16)
    suffix = _bdot(jnp.concatenate([hi, lo], axis=1), ucat)
    a = jnp.exp(log_b + suffix + carry)
    if mask is not None:
        a = jnp.where(mask, a, 0.0)
    o = _bdot(a.astype(BF16), vblk)
    total = suffix[:, 0:1] + log_1mb[:, 0:1]
    return o, carry + total


def _suffix_ones(tk):
    r = lax.broadcasted_iota(jnp.int32, (tk, tk), 0)
    c = lax.broadcasted_iota(jnp.int32, (tk, tk), 1)
    u = jnp.where(r > c, 1.0, 0.0).astype(BF16)
    return jnp.concatenate([u, u], axis=0)


def _sb_prompt_kernel(bias_ref, q_ref, k_ref, v_ref, gn_ref, o_ref):
    tq = q_ref.shape[0]
    qi = pl.program_id(1)
    ucat = _suffix_ones(tq)
    r = lax.broadcasted_iota(jnp.int32, (tq, tq), 0)
    c = lax.broadcasted_iota(jnp.int32, (tq, tq), 1)
    causal = c < r
    lo_half = lax.broadcasted_iota(jnp.int32, (tq, LANES), 1) < HEAD_DIM
    diag = pl.multiple_of(qi * tq, tq)
    outs = []
    for p in range(HEAD_PAIRS):
        lanes = slice(p * LANES, (p + 1) * LANES)
        q2 = q_ref[:, lanes].astype(F32)
        acc2 = None
        for e in range(2):
            qm = (jnp.where(lo_half, q2, 0.0) if e == 0 else jnp.where(lo_half, 0.0, q2)).astype(BF16)
            bias = bias_ref[2 * p + e]
            o0, c0 = _sb_block(qm, k_ref[pl.ds(diag, tq), lanes], v_ref[pl.ds(diag, tq), lanes],
                               bias, jnp.zeros((tq, 1), F32), ucat, causal)

            def body(i, st, qm=qm, bias=bias, lanes=lanes):
                acc, carry = st
                start = pl.multiple_of((qi - 1 - i) * tq, tq)
                o, carry = _sb_block(qm, k_ref[pl.ds(start, tq), lanes], v_ref[pl.ds(start, tq), lanes],
                                     bias, carry, ucat, None)
                return acc + o, carry

            acc, _ = lax.fori_loop(0, qi, body, (o0, c0))
            acc2 = acc if e == 0 else jnp.where(lo_half, acc2, acc)
        outs.append(acc2)
    o_ref[...] = _rms(jnp.concatenate(outs, axis=1), gn_ref[...]).astype(BF16)


def _sb_sample_kernel(pt_ref, bias_ref, q_ref, kn_ref, vn_ref, gn_ref, *refs, n_tok, n_pages):
    del pt_ref
    kpages = refs[:n_pages]
    vpages = refs[n_pages:2 * n_pages]
    o_ref = refs[2 * n_pages]
    m = n_tok * N_HEADS
    ucat = _suffix_ones(PAGE_SIZE)
    head_of_lane = lax.broadcasted_iota(jnp.int32, (N_HEADS, D_SB), 1) // HEAD_DIM
    own = head_of_lane == lax.broadcasted_iota(jnp.int32, (N_HEADS, D_SB), 0)
    q = q_ref[...]
    qbd = jnp.concatenate(
        [jnp.where(own, jnp.broadcast_to(q[t:t + 1, :], (N_HEADS, D_SB)), 0.0) for t in range(n_tok)],
        axis=0).astype(BF16)
    bias = bias_ref[...]

    pad = jnp.zeros((PAGE_SIZE - SUBLANES, D_SB), F32)
    kn = jnp.concatenate([kn_ref[...], pad], axis=0).astype(BF16)
    vn = jnp.concatenate([vn_ref[...], pad], axis=0).astype(BF16)
    tok = lax.broadcasted_iota(jnp.int32, (m, PAGE_SIZE), 0) // N_HEADS
    mask = lax.broadcasted_iota(jnp.int32, (m, PAGE_SIZE), 1) < tok
    acc, carry = _sb_block(qbd, kn, vn, bias, jnp.zeros((m, 1), F32), ucat, mask)
    for j in reversed(range(n_pages)):
        o, carry = _sb_block(qbd, kpages[j][...].astype(BF16), vpages[j][...].astype(BF16),
                             bias, carry, ucat, None)
        acc = acc + o
    for t in range(n_tok):
        blk = jnp.where(own, acc[t * N_HEADS:(t + 1) * N_HEADS, :], 0.0)
        row = jnp.sum(blk, axis=0, keepdims=True)
        o_ref[t:t + 1, :] = _rms(row, gn_ref[...])


def _mix_in(h_ref, osb_ref, oml_ref, wout_ref, gffn_ref):
    o = jnp.concatenate([osb_ref[...], oml_ref[...]], axis=1)
    h1 = h_ref[...] + _bdot(o, wout_ref[...])
    return h1, _rms(h1, gffn_ref[...]).astype(BF16)


def _ffn_chunk(hn, wup_ref, c):
    cols = slice(c * FF_CHUNK, (c + 1) * FF_CHUNK)
    gate = _bdot(hn, wup_ref[:, cols])
    up = _bdot(hn, wup_ref[:, D_FF + c * FF_CHUNK:D_FF + (c + 1) * FF_CHUNK])
    return cols, gate, up


def _ple_out(h1, act_ref, wdn_ref, p_ref, gple_ref, wg_ref, wp_ref, gfin_ref, out_ref, final):
    h2 = h1 + _bdot(act_ref[...], wdn_ref[...])
    gate = jax.nn.sigmoid(_bdot(_rms(h2, gple_ref[...]).astype(BF16), wg_ref[...]))
    h3 = h2 + gate * _bdot(p_ref[...].astype(BF16), wp_ref[...])
    out_ref[...] = _rms(h3, gfin_ref[...]) if final else h3


def _ffn_prompt_kernel(h_ref, osb_ref, oml_ref, p_ref, wout_ref, gffn_ref, wup_ref, cw_ref, cb_ref,
                       wdn_ref, gple_ref, wg_ref, wp_ref, gfin_ref,
                       out_ref, cst_ref, act_ref, carry_ref, *, tiles_per_seq, final):
    tm = h_ref.shape[0]
    h1, hn = _mix_in(h_ref, osb_ref, oml_ref, wout_ref, gffn_ref)

    @pl.when(pl.program_id(0) % tiles_per_seq == 0)
    def _():
        carry_ref[...] = jnp.zeros_like(carry_ref)

    rowi = lax.broadcasted_iota(jnp.int32, (SUBLANES, FF_CHUNK), 0)
    for c in range(N_FF_CHUNKS):
        cols, gate, up = _ffn_chunk(hn, wup_ref, c)
        prev = carry_ref[:, cols]
        carry_ref[:, cols] = gate[tm - SUBLANES:tm, :]
        cst_ref[:, cols] = gate[tm - (CONV_W - 1):tm, :]
        conv = cb_ref[:, cols] + cw_ref[CONV_W - 1:CONV_W, cols] * gate
        for d in range(1, CONV_W):
            rolled = pltpu.roll(gate, d, axis=0)
            top = jnp.where(rowi < d, pltpu.roll(prev, d, axis=0), rolled[0:SUBLANES])
            shifted = jnp.concatenate([top, rolled[SUBLANES:]], axis=0)
            conv = conv + cw_ref[CONV_W - 1 - d:CONV_W - d, cols] * shifted
        act_ref[:, cols] = (_gelu(conv) * up).astype(BF16)
    _ple_out(h1, act_ref, wdn_ref, p_ref, gple_ref, wg_ref, wp_ref, gfin_ref, out_ref, final)


def _ffn_sample_kernel(h_ref, osb_ref, oml_ref, p_ref, pre_ref, wout_ref, gffn_ref, wup_ref, cw_ref, cb_ref,
                       wdn_ref, gple_ref, wg_ref, wp_ref, gfin_ref,
                       out_ref, cst_ref, act_ref, *, n_tok, n_seq, final):
    h1, hn = _mix_in(h_ref, osb_ref, oml_ref, wout_ref, gffn_ref)
    n_pre = CONV_W - 1
    for c in range(N_FF_CHUNKS):
        cols, gate, up = _ffn_chunk(hn, wup_ref, c)
        gp = [pre_ref[j * n_seq:(j + 1) * n_seq, cols] for j in range(n_pre)]
        gp += [gate[t * n_seq:(t + 1) * n_seq] for t in range(n_tok)]
        for j in range(n_pre):
            cst_ref[j * n_seq:(j + 1) * n_seq, cols] = gp[n_tok + j]
        conv = []
        for t in range(n_tok):
            acc = cb_ref[:, cols]
            for j in range(CONV_W):
                acc = acc + cw_ref[j:j + 1, cols] * gp[t + j]
            conv.append(acc)
        act_ref[:, cols] = (_gelu(jnp.concatenate(conv, axis=0)) * up).astype(BF16)
    _ple_out(h1, act_ref, wdn_ref, p_ref, gple_ref, wg_ref, wp_ref, gfin_ref, out_ref, final)


def _params(*sem):
    return pltpu.CompilerParams(dimension_semantics=sem, vmem_limit_bytes=VMEM_LIMIT)


def _proj_prompt(h, g, win, ws_pair, bias_full, gnm):
    n = h.shape[0]
    tile = lambda w: pl.BlockSpec((TM, w), lambda i: (i, 0))
    return pl.pallas_call(
        _proj_prompt_kernel,
        grid=(n // TM,),
        in_specs=[tile(D_MODEL), _resident((1, D_MODEL)), _resident((D_MODEL, D_IN)),
                  _resident(ws_pair.shape), _resident(bias_full.shape), _resident((1, D_MLP))],
        out_specs=[tile(D_SB)] * 6,
        out_shape=[jax.ShapeDtypeStruct((n, D_SB), BF16), jax.ShapeDtypeStruct((n, D_SB), F32),
                   jax.ShapeDtypeStruct((n, D_SB), F32), jax.ShapeDtypeStruct((n, D_SB), BF16),
                   jax.ShapeDtypeStruct((n, D_SB), BF16), jax.ShapeDtypeStruct((n, D_MLP), BF16)],
        compiler_params=_params("arbitrary"),
        name="proj_prompt",
    )(h, g, win, ws_pair, bias_full, gnm)


def _proj_sample(h, g, win, wrow, brow, gnm, n_tok, n_seq):
    n = h.shape[0]
    full = lambda w: pl.BlockSpec((n, w), lambda i: (0, 0))
    return pl.pallas_call(
        functools.partial(_proj_sample_kernel, n_tok=n_tok, n_seq=n_seq),
        grid=(1,),
        in_specs=[full(D_MODEL), _resident((1, D_MODEL)), _resident((D_MODEL, D_IN)),
                  _resident(wrow.shape), _resident(brow.shape), _resident((1, D_MLP))],
        out_specs=[full(D_SB)] * 5,
        out_shape=[jax.ShapeDtypeStruct((n, D_SB), F32)] * 4 + [jax.ShapeDtypeStruct((n, D_MLP), BF16)],
        compiler_params=_params("arbitrary"),
        name="proj_sample",
    )(h, g, win, wrow, brow, gnm)


def _sb_prompt(bias, q, kb, vb, gn, seq):
    n = q.shape[0]
    nq = seq // TQ
    return pl.pallas_call(
        _sb_prompt_kernel,
        grid=(n // seq, nq),
        in_specs=[pl.BlockSpec(memory_space=pltpu.SMEM),
                  pl.BlockSpec((TQ, D_SB), lambda b, i: (b * nq + i, 0)),
                  pl.BlockSpec((seq, D_SB), lambda b, i: (b, 0)),
                  pl.BlockSpec((seq, D_SB), lambda b, i: (b, 0)),
                  _resident((1, D_SB))],
        out_specs=pl.BlockSpec((TQ, D_SB), lambda b, i: (b * nq + i, 0)),
        out_shape=jax.ShapeDtypeStruct((n, D_SB), BF16),
        compiler_params=_params("arbitrary", "arbitrary"),
        name="sb_prompt",
    )(bias, q, kb, vb, gn)


def _sb_sample(page_table, bias_rows, q, k_new, v_new, gn, cache_k, cache_v):
    n_seq, n_pages = page_table.shape
    n_tok = q.shape[1]
    page = lambda j: pl.BlockSpec((None, PAGE_SIZE, D_SB), lambda b, pt, j=j: (pt[b, j], 0, 0))
    per_seq = lambda rows: pl.BlockSpec((None, rows, D_SB), lambda b, pt: (b, 0, 0))
    const = lambda shape: pl.BlockSpec(shape, lambda b, pt: (0,) * len(shape), pipeline_mode=pl.Buffered(1))
    grid_spec = pltpu.PrefetchScalarGridSpec(
        num_scalar_prefetch=1,
        grid=(n_seq,),
        in_specs=[const(bias_rows.shape), per_seq(n_tok), per_seq(SUBLANES), per_seq(SUBLANES),
                  const((1, D_SB))] + [page(j) for j in range(n_pages)] * 2,
        out_specs=per_seq(n_tok),
    )
    return pl.pallas_call(
        functools.partial(_sb_sample_kernel, n_tok=n_tok, n_pages=n_pages),
        grid_spec=grid_spec,
        out_shape=jax.ShapeDtypeStruct((n_seq, n_tok, D_SB), F32),
        compiler_params=_params("arbitrary"),
        name="sb_sample",
    )(page_table, bias_rows, q, k_new, v_new, gn, *([cache_k] * n_pages), *([cache_v] * n_pages))


def _weight_specs():
    return [_resident((D_MODEL, D_MODEL)), _resident((1, D_MODEL)), _resident((D_MODEL, 2 * D_FF)),
            _resident((CONV_W, D_FF)), _resident((1, D_FF)), _resident((D_FF, D_MODEL)),
            _resident((1, D_MODEL)), _resident((D_MODEL, D_MODEL)), _resident((PLE_DIM, D_MODEL)),
            _resident((1, D_MODEL))]


def _ffn_prompt(h, osb, oml, p, weights, seq, final):
    n = h.shape[0]
    tile = lambda w: pl.BlockSpec((TM, w), lambda i: (i, 0))
    tiles_per_seq = seq // TM
    return pl.pallas_call(
        functools.partial(_ffn_prompt_kernel, tiles_per_seq=tiles_per_seq, final=final),
        grid=(n // TM,),
        in_specs=[tile(D_MODEL), tile(D_SB), tile(D_MLP), tile(PLE_DIM)] + _weight_specs(),
        out_specs=[tile(D_MODEL),
                   pl.BlockSpec((None, CONV_W - 1, D_FF), lambda i: (i // tiles_per_seq, 0, 0))],
        out_shape=[jax.ShapeDtypeStruct((n, D_MODEL), F32),
                   jax.ShapeDtypeStruct((n // seq, CONV_W - 1, D_FF), F32)],
        scratch_shapes=[pltpu.VMEM((TM, D_FF), BF16), pltpu.VMEM((SUBLANES, D_FF), F32)],
        compiler_params=_params("arbitrary"),
        name="ffn_prompt",
    )(h, osb, oml, p, *weights)


def _ffn_sample(h, osb, oml, p, prefix, weights, n_tok, n_seq, final):
    n = h.shape[0]
    full = lambda rows, w: pl.BlockSpec((rows, w), lambda i: (0, 0))
    n_pre = (CONV_W - 1) * n_seq
    return pl.pallas_call(
        functools.partial(_ffn_sample_kernel, n_tok=n_tok, n_seq=n_seq, final=final),
        grid=(1,),
        in_specs=[full(n, D_MODEL), full(n, D_SB), full(n, D_MLP), full(n, PLE_DIM),
                  full(n_pre, D_FF)] + _weight_specs(),
        out_specs=[full(n, D_MODEL), full(n_pre, D_FF)],
        out_shape=[jax.ShapeDtypeStruct((n, D_MODEL), F32), jax.ShapeDtypeStruct((n_pre, D_FF), F32)],
        scratch_shapes=[pltpu.VMEM((n, D_FF), BF16)],
        compiler_params=_params("arbitrary"),
        name="ffn_sample",
    )(h, osb, oml, p, prefix, *weights)


def kernel(x_prompt, x_sample, cache_k, cache_v, state_conv, page_table, p_prompt, p_sample, norm_mix_g, w_in, sb_bias, gn_sb, gn_mlp, w_spatial, b_spatial, w_out, norm_ffn_g, w_up, conv_w, conv_b, w_down, norm_ple_g, w_ple_gate, w_ple_proj, final_norm_g):
    depth = w_in.shape[0]
    bp, seq, _ = x_prompt.shape
    n_seq, n_tok, _ = x_sample.shape
    n_phys = cache_k.shape[1]

    to_tok_major = lambda x: jnp.swapaxes(x, 0, 1).reshape((x.shape[0] * x.shape[1],) + x.shape[2:])
    to_seq_major = lambda x, t: jnp.swapaxes(x.reshape((t, n_seq) + x.shape[1:]), 0, 1)
    row = lambda x: x.reshape(1, -1)

    hp = x_prompt.reshape(bp * seq, D_MODEL)
    hs = to_tok_major(x_sample)
    ck = cache_k.reshape(depth, n_phys, PAGE_SIZE, D_SB)
    cv = cache_v.reshape(depth, n_phys, PAGE_SIZE, D_SB)
    gfin = row(final_norm_g)

    kp_l, vp_l, ks_l, vs_l, cp_l, cs_l, ms_l = [], [], [], [], [], [], []
    for i in range(depth):
        final = i == depth - 1
        win = w_in[i].astype(BF16)
        weights = (w_out[i].astype(BF16), row(norm_ffn_g[i]), w_up[i].astype(BF16), conv_w[i], row(conv_b[i]),
                   w_down[i].astype(BF16), row(norm_ple_g[i]), w_ple_gate[i].astype(BF16),
                   w_ple_proj[i].astype(BF16), gfin)
        gmix, gnm, gns = row(norm_mix_g[i]), row(gn_mlp[i]), row(gn_sb[i])

        ws = w_spatial[i]
        ws_pair = jnp.concatenate([ws[0::2], ws[1::2]], axis=2)
        bias_full = jnp.repeat(b_spatial[i].T, GROUP_DIM, axis=1)
        q, k, v, kb, vb, oml = _proj_prompt(hp, gmix, win, ws_pair, bias_full, gnm)
        osb = _sb_prompt(sb_bias[i], q, kb, vb, gns, seq)
        hp, conv_p = _ffn_prompt(hp, osb, oml, p_prompt[i].reshape(bp * seq, PLE_DIM), weights, seq, final)
        kp_l.append(k); vp_l.append(v); cp_l.append(conv_p)

        wrow = jnp.repeat(ws[:, :n_tok, :n_tok].transpose(1, 2, 0).reshape(n_tok * n_tok, N_GROUPS),
                          GROUP_DIM, axis=1)
        brow = jnp.repeat(b_spatial[i][:, :n_tok].T, GROUP_DIM, axis=1)
        q, k, v, vm, oml = _proj_sample(hs, gmix, win, wrow, brow, gnm, n_tok, n_seq)
        pad_rows = lambda x: jnp.pad(to_seq_major(x, n_tok), ((0, 0), (0, SUBLANES - n_tok), (0, 0)))
        bias_rows = jnp.broadcast_to(jnp.tile(sb_bias[i], n_tok)[:, None], (n_tok * N_HEADS, LANES))
        osb = _sb_sample(page_table, bias_rows, to_seq_major(q, n_tok), pad_rows(k), pad_rows(v), gns,
                         ck[i], cv[i])
        osb = to_tok_major(osb).astype(BF16)
        hs, conv_s = _ffn_sample(hs, osb, oml, to_tok_major(p_sample[i]), to_tok_major(state_conv[i]),
                                 weights, n_tok, n_seq, final)
        ks_l.append(to_seq_major(k, n_tok)); vs_l.append(to_seq_major(v, n_tok))
        cs_l.append(to_seq_major(conv_s, CONV_W - 1)); ms_l.append(to_seq_major(vm, n_tok))

    heads = lambda x: x.reshape(x.shape[:-1] + (N_HEADS, HEAD_DIM))
    return (hp.reshape(bp, seq, D_MODEL),
            to_seq_major(hs, n_tok),
            heads(jnp.stack(kp_l).reshape(depth, bp, seq, D_SB)),
            heads(jnp.stack(vp_l).reshape(depth, bp, seq, D_SB)),
            heads(jnp.stack(ks_l)),
            heads(jnp.stack(vs_l)),
            jnp.stack(cp_l),
            jnp.stack(cs_l),
            jnp.stack(ms_l))
```

```python
import functools
import math

import jax
import jax.numpy as jnp
from jax import lax
from jax.experimental import pallas as pl
from jax.experimental.pallas import tpu as pltpu

F32 = jnp.float32
BF16 = jnp.bfloat16

D_MODEL = 1024
N_HEADS = 8
HEAD_DIM = 64
D_SB = N_HEADS * HEAD_DIM
N_GROUPS = 8
GROUP_DIM = 64
D_MLP = N_GROUPS * GROUP_DIM
D_IN = 3 * D_SB + 2 * D_MLP
CHUNK = 128
D_FF = 2816
CONV_W = 3
PLE_DIM = 256
PAGE_SIZE = 128
EPS = 1e-6

LANES = 128
SUBLANES = 8
HEAD_PAIRS = D_SB // LANES
FF_CHUNK = 256
N_FF_CHUNKS = D_FF // FF_CHUNK
TM = 512
TQ = 256
VMEM_LIMIT = 56 * 1024 * 1024


def _bdot(a, b):
    return jnp.dot(a, b, preferred_element_type=F32)


def _rms(x, g):
    ms = jnp.mean(x * x, axis=-1, keepdims=True)
    return x * lax.rsqrt(ms + EPS) * g


def _gelu(x):
    c = math.sqrt(2.0 / math.pi)
    return x * (0.5 * (1.0 + jnp.tanh(c * (x + 0.044715 * (x * x * x)))))


def _resident(shape):
    nd = len(shape)
    return pl.BlockSpec(shape, lambda *_: (0,) * nd, pipeline_mode=pl.Buffered(1))


def _project(h_ref, g_ref, win_ref):
    hn = _rms(h_ref[...], g_ref[...]).astype(BF16)
    seg = lambda j: _bdot(hn, win_ref[:, j * D_SB:(j + 1) * D_SB])
    return seg(0) * (HEAD_DIM ** -0.5), seg(1), seg(2), _gelu(seg(3)), _gelu(seg(4))


def _proj_prompt_kernel(h_ref, g_ref, win_ref, ws_ref, bs_ref, gnm_ref,
                        q_ref, k_ref, v_ref, kb_ref, vb_ref, om_ref):
    q, k, v, u, vm = _project(h_ref, g_ref, win_ref)
    q_ref[...] = q.astype(BF16)
    k_ref[...] = k
    v_ref[...] = v
    kb_ref[...] = k.astype(BF16)
    vb_ref[...] = v.astype(BF16)

    row = lax.broadcasted_iota(jnp.int32, (CHUNK, 2 * CHUNK), 0)
    col = lax.broadcasted_iota(jnp.int32, (CHUNK, 2 * CHUNK), 1)
    tril = (col & (CHUNK - 1)) <= row
    wpair = [jnp.where(tril, ws_ref[p], 0.0).astype(BF16) for p in range(HEAD_PAIRS)]
    lo_half = lax.broadcasted_iota(jnp.int32, (CHUNK, LANES), 1) < GROUP_DIM
    for c in range(h_ref.shape[0] // CHUNK):
        rows = slice(c * CHUNK, (c + 1) * CHUNK)
        parts = []
        for p in range(HEAD_PAIRS):
            s = vm[rows, p * LANES:(p + 1) * LANES]
            rhs = jnp.concatenate([jnp.where(lo_half, s, 0.0).astype(BF16),
                                   jnp.where(lo_half, 0.0, s).astype(BF16)], axis=0)
            parts.append(_bdot(wpair[p], rhs))
        mixed = jnp.concatenate(parts, axis=1) + bs_ref[...]
        om_ref[rows, :] = _rms(u[rows] * mixed, gnm_ref[...]).astype(BF16)


def _proj_sample_kernel(h_ref, g_ref, win_ref, wrow_ref, brow_ref, gnm_ref,
                        q_ref, k_ref, v_ref, vm_ref, om_ref, *, n_tok, n_seq):
    q, k, v, u, vm = _project(h_ref, g_ref, win_ref)
    q_ref[...] = q
    k_ref[...] = k
    v_ref[...] = v
    vm_ref[...] = vm
    for t in range(n_tok):
        mixed = brow_ref[t:t + 1, :]
        for s in range(t + 1):
            w = wrow_ref[t * n_tok + s:t * n_tok + s + 1, :]
            mixed = mixed + w * vm[s * n_seq:(s + 1) * n_seq]
        rows = slice(t * n_seq, (t + 1) * n_seq)
        om_ref[rows, :] = _rms(u[rows] * mixed, gnm_ref[...]).astype(BF16)


def _dot_nt(a, b):
    return lax.dot_general(a, b, (((1,), (1,)), ((), ())), preferred_element_type=F32)


def _sb_weights(z, carry, ucat, mask):
    lg = jnp.log(1.0 + jnp.exp(-jnp.abs(z)))
    log_b = jnp.minimum(z, 0.0) - lg
    log_1mb = log_b - z
    if mask is not None:
        log_1mb = jnp.where(mask, log_1mb, 0.0)
    hi = log_1mb.astype(BF16)
    lo = (log_1mb - hi.astype(F32)).astype(BF16)
    suffix = _bdot(jnp.concatenate([hi, lo], axis=1), ucat)
    a = jnp.exp(log_b + suffix + carry)
    if mask is not None:
        a = jnp.where(mask, a, 0.0)
    total = suffix[:, 0:1] + log_1mb[:, 0:1]
    return a, carry + total


def _suffix_ones(tk):
    r = lax.broadcasted_iota(jnp.int32, (tk, tk), 0)
    c = lax.broadcasted_iota(jnp.int32, (tk, tk), 1)
    u = jnp.where(r > c, 1.0, 0.0).astype(BF16)
    return jnp.concatenate([u, u], axis=0)


def _sb_prompt_kernel(bias_ref, q_ref, k_ref, v_ref, gn_ref, o_ref, qm_ref, acc_ref):
    tq = q_ref.shape[0]
    qi = pl.program_id(1)
    ucat = _suffix_ones(tq)
    r = lax.broadcasted_iota(jnp.int32, (2 * tq, tq), 0)
    c = lax.broadcasted_iota(jnp.int32, (2 * tq, tq), 1)
    causal = c < (r & (tq - 1))
    lo_half = lax.broadcasted_iota(jnp.int32, (tq, LANES), 1) < HEAD_DIM
    pair_lanes = [slice(p * LANES, (p + 1) * LANES) for p in range(HEAD_PAIRS)]

    for p in range(HEAD_PAIRS):
        q2 = q_ref[:, pair_lanes[p]].astype(F32)
        qm_ref[p] = jnp.concatenate([jnp.where(lo_half, q2, 0.0), jnp.where(lo_half, 0.0, q2)],
                                    axis=0).astype(BF16)

    def pair_block(p, start, carry, mask):
        s = _dot_nt(qm_ref[p], k_ref[pl.ds(start, tq), pair_lanes[p]])
        z = jnp.concatenate([s[:tq] + bias_ref[2 * p], s[tq:] + bias_ref[2 * p + 1]], axis=0)
        a, carry = _sb_weights(z, carry, ucat, mask)
        o = _bdot(a.astype(BF16), v_ref[pl.ds(start, tq), pair_lanes[p]])
        return jnp.where(lo_half, o[:tq], o[tq:]), carry

    diag = pl.multiple_of(qi * tq, tq)
    carries = []
    for p in range(HEAD_PAIRS):
        o, carry = pair_block(p, diag, jnp.zeros((2 * tq, 1), F32), causal)
        acc_ref[:, pair_lanes[p]] = o
        carries.append(carry)

    def body(i, carries):
        start = pl.multiple_of((qi - 1 - i) * tq, tq)
        out = []
        for p in range(HEAD_PAIRS):
            o, carry = pair_block(p, start, carries[p], None)
            acc_ref[:, pair_lanes[p]] += o
            out.append(carry)
        return tuple(out)

    lax.fori_loop(0, qi, body, tuple(carries))
    o_ref[...] = _rms(acc_ref[...], gn_ref[...]).astype(BF16)


def _sb_sample_kernel(pt_ref, bias_ref, q_ref, kn_ref, vn_ref, gn_ref, *refs, n_tok, n_pages):
    del pt_ref
    kpages = refs[:n_pages]
    vpages = refs[n_pages:2 * n_pages]
    o_ref, kt_ref, vt_ref = refs[2 * n_pages:]
    m = n_tok * N_HEADS
    tk = 2 * PAGE_SIZE
    past = n_pages * PAGE_SIZE
    head_of_lane = lax.broadcasted_iota(jnp.int32, (N_HEADS, D_SB), 1) // HEAD_DIM
    own = head_of_lane == lax.broadcasted_iota(jnp.int32, (N_HEADS, D_SB), 0)
    q = q_ref[...]
    qbd = jnp.concatenate(
        [jnp.where(own, jnp.broadcast_to(q[t:t + 1, :], (N_HEADS, D_SB)), 0.0) for t in range(n_tok)],
        axis=0).astype(BF16)
    bias = bias_ref[:, 0:1]

    for j in range(n_pages):
        kt_ref[:, j * PAGE_SIZE:(j + 1) * PAGE_SIZE] = kpages[j][...].astype(BF16)
        vt_ref[:, j * PAGE_SIZE:(j + 1) * PAGE_SIZE] = vpages[j][...].astype(BF16)

    pad = jnp.zeros((PAGE_SIZE - SUBLANES, D_SB), F32)
    kn = jnp.concatenate([kn_ref[...], pad], axis=0).astype(BF16)
    vn = jnp.concatenate([vn_ref[...], pad], axis=0).astype(BF16)
    tok = lax.broadcasted_iota(jnp.int32, (m, PAGE_SIZE), 0) // N_HEADS
    mask = lax.broadcasted_iota(jnp.int32, (m, PAGE_SIZE), 1) < tok
    a, carry = _sb_weights(_dot_nt(qbd, kn) + bias, jnp.zeros((m, 1), F32), _suffix_ones(PAGE_SIZE), mask)
    acc = _bdot(a.astype(BF16), vn)

    z = _bdot(qbd, kt_ref[...]) + bias
    ucat = _suffix_ones(tk)
    blocks = [None] * (past // tk)
    for j in reversed(range(past // tk)):
        a, carry = _sb_weights(z[:, j * tk:(j + 1) * tk], carry, ucat, None)
        blocks[j] = a.astype(BF16)
    acc = acc + _dot_nt(jnp.concatenate(blocks, axis=1), vt_ref[...])

    for t in range(n_tok):
        blk = jnp.where(own, acc[t * N_HEADS:(t + 1) * N_HEADS, :], 0.0)
        row = jnp.sum(blk, axis=0, keepdims=True)
        o_ref[t:t + 1, :] = _rms(row, gn_ref[...])


def _mix_in(h_ref, osb_ref, oml_ref, wout_ref, gffn_ref):
    o = jnp.concatenate([osb_ref[...], oml_ref[...]], axis=1)
    h1 = h_ref[...] + _bdot(o, wout_ref[...])
    return h1, _rms(h1, gffn_ref[...]).astype(BF16)


def _ffn_chunk(hn, wup_ref, c):
    cols = slice(c * FF_CHUNK, (c + 1) * FF_CHUNK)
    gate = _bdot(hn, wup_ref[:, cols])
    up = _bdot(hn, wup_ref[:, D_FF + c * FF_CHUNK:D_FF + (c + 1) * FF_CHUNK])
    return cols, gate, up


def _ple_out(h1, act_ref, wdn_ref, p_ref, gple_ref, wg_ref, wp_ref, gfin_ref, out_ref, final):
    h2 = h1 + _bdot(act_ref[...], wdn_ref[...])
    gate = jax.nn.sigmoid(_bdot(_rms(h2, gple_ref[...]).astype(BF16), wg_ref[...]))
    h3 = h2 + gate * _bdot(p_ref[...].astype(BF16), wp_ref[...])
    out_ref[...] = _rms(h3, gfin_ref[...]) if final else h3


def _ffn_prompt_kernel(h_ref, osb_ref, oml_ref, p_ref, wout_ref, gffn_ref, wup_ref, cw_ref, cb_ref,
                       wdn_ref, gple_ref, wg_ref, wp_ref, gfin_ref,
                       out_ref, cst_ref, act_ref, carry_ref, *, tiles_per_seq, final):
    tm = h_ref.shape[0]
    h1, hn = _mix_in(h_ref, osb_ref, oml_ref, wout_ref, gffn_ref)

    @pl.when(pl.program_id(0) % tiles_per_seq == 0)
    def _():
        carry_ref[...] = jnp.zeros_like(carry_ref)

    rowi = lax.broadcasted_iota(jnp.int32, (SUBLANES, FF_CHUNK), 0)
    for c in range(N_FF_CHUNKS):
        cols, gate, up = _ffn_chunk(hn, wup_ref, c)
        prev = carry_ref[:, cols]
        carry_ref[:, cols] = gate[tm - SUBLANES:tm, :]
        cst_ref[:, cols] = gate[tm - (CONV_W - 1):tm, :]
        conv = cb_ref[:, cols] + cw_ref[CONV_W - 1:CONV_W, cols] * gate
        for d in range(1, CONV_W):
            rolled = pltpu.roll(gate, d, axis=0)
            top = jnp.where(rowi < d, pltpu.roll(prev, d, axis=0), rolled[0:SUBLANES])
            shifted = jnp.concatenate([top, rolled[SUBLANES:]], axis=0)
            conv = conv + cw_ref[CONV_W - 1 - d:CONV_W - d, cols] * shifted
        act_ref[:, cols] = (_gelu(conv) * up).astype(BF16)
    _ple_out(h1, act_ref, wdn_ref, p_ref, gple_ref, wg_ref, wp_ref, gfin_ref, out_ref, final)


def _ffn_sample_kernel(h_ref, osb_ref, oml_ref, p_ref, pre_ref, wout_ref, gffn_ref, wup_ref, cw_ref, cb_ref,
                       wdn_ref, gple_ref, wg_ref, wp_ref, gfin_ref,
                       out_ref, cst_ref, act_ref, *, n_tok, n_seq, final):
    h1, hn = _mix_in(h_ref, osb_ref, oml_ref, wout_ref, gffn_ref)
    n_pre = CONV_W - 1
    for c in range(N_FF_CHUNKS):
        cols, gate, up = _ffn_chunk(hn, wup_ref, c)
        gp = [pre_ref[j * n_seq:(j + 1) * n_seq, cols] for j in range(n_pre)]
        gp += [gate[t * n_seq:(t + 1) * n_seq] for t in range(n_tok)]
        for j in range(n_pre):
            cst_ref[j * n_seq:(j + 1) * n_seq, cols] = gp[n_tok + j]
        conv = []
        for t in range(n_tok):
            acc = cb_ref[:, cols]
            for j in range(CONV_W):
                acc = acc + cw_ref[j:j + 1, cols] * gp[t + j]
            conv.append(acc)
        act_ref[:, cols] = (_gelu(jnp.concatenate(conv, axis=0)) * up).astype(BF16)
    _ple_out(h1, act_ref, wdn_ref, p_ref, gple_ref, wg_ref, wp_ref, gfin_ref, out_ref, final)


def _params(*sem):
    return pltpu.CompilerParams(dimension_semantics=sem, vmem_limit_bytes=VMEM_LIMIT)


def _proj_prompt(h, g, win, ws_pair, bias_full, gnm):
    n = h.shape[0]
    tile = lambda w: pl.BlockSpec((TM, w), lambda i: (i, 0))
    return pl.pallas_call(
        _proj_prompt_kernel,
        grid=(n // TM,),
        in_specs=[tile(D_MODEL), _resident((1, D_MODEL)), _resident((D_MODEL, D_IN)),
                  _resident(ws_pair.shape), _resident(bias_full.shape), _resident((1, D_MLP))],
        out_specs=[tile(D_SB)] * 6,
        out_shape=[jax.ShapeDtypeStruct((n, D_SB), BF16), jax.ShapeDtypeStruct((n, D_SB), F32),
                   jax.ShapeDtypeStruct((n, D_SB), F32), jax.ShapeDtypeStruct((n, D_SB), BF16),
                   jax.ShapeDtypeStruct((n, D_SB), BF16), jax.ShapeDtypeStruct((n, D_MLP), BF16)],
        compiler_params=_params("arbitrary"),
        name="proj_prompt",
    )(h, g, win, ws_pair, bias_full, gnm)


def _proj_sample(h, g, win, wrow, brow, gnm, n_tok, n_seq):
    n = h.shape[0]
    full = lambda w: pl.BlockSpec((n, w), lambda i: (0, 0))
    return pl.pallas_call(
        functools.partial(_proj_sample_kernel, n_tok=n_tok, n_seq=n_seq),
        grid=(1,),
        in_specs=[full(D_MODEL), _resident((1, D_MODEL)), _resident((D_MODEL, D_IN)),
                  _resident(wrow.shape), _resident(brow.shape), _resident((1, D_MLP))],
        out_specs=[full(D_SB)] * 5,
        out_shape=[jax.ShapeDtypeStruct((n, D_SB), F32)] * 4 + [jax.ShapeDtypeStruct((n, D_MLP), BF16)],
        compiler_params=_params("arbitrary"),
        name="proj_sample",
    )(h, g, win, wrow, brow, gnm)


def _sb_prompt(bias, q, kb, vb, gn, seq):
    n = q.shape[0]
    nq = seq // TQ
    return pl.pallas_call(
        _sb_prompt_kernel,
        grid=(n // seq, nq),
        in_specs=[pl.BlockSpec(memory_space=pltpu.SMEM),
                  pl.BlockSpec((TQ, D_SB), lambda b, i: (b * nq + i, 0)),
                  pl.BlockSpec((seq, D_SB), lambda b, i: (b, 0)),
                  pl.BlockSpec((seq, D_SB), lambda b, i: (b, 0)),
                  _resident((1, D_SB))],
        out_specs=pl.BlockSpec((TQ, D_SB), lambda b, i: (b * nq + i, 0)),
        out_shape=jax.ShapeDtypeStruct((n, D_SB), BF16),
        scratch_shapes=[pltpu.VMEM((HEAD_PAIRS, 2 * TQ, LANES), BF16), pltpu.VMEM((TQ, D_SB), F32)],
        compiler_params=_params("arbitrary", "arbitrary"),
        name="sb_prompt",
    )(bias, q, kb, vb, gn)


def _sb_sample(page_table, bias_rows, q, k_new, v_new, gn, cache_kt, cache_vt, layer):
    n_seq, n_pages = page_table.shape
    n_tok = q.shape[1]
    page = lambda j: pl.BlockSpec((None, None, D_SB, PAGE_SIZE), lambda b, pt, j=j: (layer, pt[b, j], 0, 0))
    per_seq = lambda rows: pl.BlockSpec((None, rows, D_SB), lambda b, pt: (b, 0, 0))
    const = lambda shape: pl.BlockSpec(shape, lambda b, pt: (0,) * len(shape), pipeline_mode=pl.Buffered(1))
    grid_spec = pltpu.PrefetchScalarGridSpec(
        num_scalar_prefetch=1,
        grid=(n_seq,),
        in_specs=[const(bias_rows.shape), per_seq(n_tok), per_seq(SUBLANES), per_seq(SUBLANES),
                  const((1, D_SB))] + [page(j) for j in range(n_pages)] * 2,
        out_specs=per_seq(n_tok),
        scratch_shapes=[pltpu.VMEM((D_SB, n_pages * PAGE_SIZE), BF16)] * 2,
    )
    return pl.pallas_call(
        functools.partial(_sb_sample_kernel, n_tok=n_tok, n_pages=n_pages),
        grid_spec=grid_spec,
        out_shape=jax.ShapeDtypeStruct((n_seq, n_tok, D_SB), F32),
        compiler_params=_params("arbitrary"),
        name="sb_sample",
    )(page_table, bias_rows, q, k_new, v_new, gn, *([cache_kt] * n_pages), *([cache_vt] * n_pages))


def _weight_specs():
    return [_resident((D_MODEL, D_MODEL)), _resident((1, D_MODEL)), _resident((D_MODEL, 2 * D_FF)),
            _resident((CONV_W, D_FF)), _resident((1, D_FF)), _resident((D_FF, D_MODEL)),
            _resident((1, D_MODEL)), _resident((D_MODEL, D_MODEL)), _resident((PLE_DIM, D_MODEL)),
            _resident((1, D_MODEL))]


def _ffn_prompt(h, osb, oml, p, weights, seq, final):
    n = h.shape[0]
    tile = lambda w: pl.BlockSpec((TM, w), lambda i: (i, 0))
    tiles_per_seq = seq // TM
    return pl.pallas_call(
        functools.partial(_ffn_prompt_kernel, tiles_per_seq=tiles_per_seq, final=final),
        grid=(n // TM,),
        in_specs=[tile(D_MODEL), tile(D_SB), tile(D_MLP), tile(PLE_DIM)] + _weight_specs(),
        out_specs=[tile(D_MODEL),
                   pl.BlockSpec((None, CONV_W - 1, D_FF), lambda i: (i // tiles_per_seq, 0, 0))],
        out_shape=[jax.ShapeDtypeStruct((n, D_MODEL), F32),
                   jax.ShapeDtypeStruct((n // seq, CONV_W - 1, D_FF), F32)],
        scratch_shapes=[pltpu.VMEM((TM, D_FF), BF16), pltpu.VMEM((SUBLANES, D_FF), F32)],
        compiler_params=_params("arbitrary"),
        name="ffn_prompt",
    )(h, osb, oml, p, *weights)


def _ffn_sample(h, osb, oml, p, prefix, weights, n_tok, n_seq, final):
    n = h.shape[0]
    full = lambda rows, w: pl.BlockSpec((rows, w), lambda i: (0, 0))
    n_pre = (CONV_W - 1) * n_seq
    return pl.pallas_call(
        functools.partial(_ffn_sample_kernel, n_tok=n_tok, n_seq=n_seq, final=final),
        grid=(1,),
        in_specs=[full(n, D_MODEL), full(n, D_SB), full(n, D_MLP), full(n, PLE_DIM),
                  full(n_pre, D_FF)] + _weight_specs(),
        out_specs=[full(n, D_MODEL), full(n_pre, D_FF)],
        out_shape=[jax.ShapeDtypeStruct((n, D_MODEL), F32), jax.ShapeDtypeStruct((n_pre, D_FF), F32)],
        scratch_shapes=[pltpu.VMEM((n, D_FF), BF16)],
        compiler_params=_params("arbitrary"),
        name="ffn_sample",
    )(h, osb, oml, p, prefix, *weights)


def kernel(x_prompt, x_sample, cache_k, cache_v, state_conv, page_table, p_prompt, p_sample, norm_mix_g, w_in, sb_bias, gn_sb, gn_mlp, w_spatial, b_spatial, w_out, norm_ffn_g, w_up, conv_w, conv_b, w_down, norm_ple_g, w_ple_gate, w_ple_proj, final_norm_g):
    depth = w_in.shape[0]
    bp, seq, _ = x_prompt.shape
    n_seq, n_tok, _ = x_sample.shape
    n_phys = cache_k.shape[1]

    to_tok_major = lambda x: jnp.swapaxes(x, 0, 1).reshape((x.shape[0] * x.shape[1],) + x.shape[2:])
    to_seq_major = lambda x, t: jnp.swapaxes(x.reshape((t, n_seq) + x.shape[1:]), 0, 1)
    row = lambda x: x.reshape(1, -1)

    hp = x_prompt.reshape(bp * seq, D_MODEL)
    hs = to_tok_major(x_sample)
    page_t = lambda c: jnp.transpose(c, (0, 1, 3, 4, 2)).reshape(depth, n_phys, D_SB, PAGE_SIZE)
    ckt, cvt = page_t(cache_k), page_t(cache_v)
    gfin = row(final_norm_g)

    kp_l, vp_l, ks_l, vs_l, cp_l, cs_l, ms_l = [], [], [], [], [], [], []
    for i in range(depth):
        final = i == depth - 1
        win = w_in[i].astype(BF16)
        weights = (w_out[i].astype(BF16), row(norm_ffn_g[i]), w_up[i].astype(BF16), conv_w[i], row(conv_b[i]),
                   w_down[i].astype(BF16), row(norm_ple_g[i]), w_ple_gate[i].astype(BF16),
                   w_ple_proj[i].astype(BF16), gfin)
        gmix, gnm, gns = row(norm_mix_g[i]), row(gn_mlp[i]), row(gn_sb[i])

        ws = w_spatial[i]
        ws_pair = jnp.concatenate([ws[0::2], ws[1::2]], axis=2)
        bias_full = jnp.repeat(b_spatial[i].T, GROUP_DIM, axis=1)
        q, k, v, kb, vb, oml = _proj_prompt(hp, gmix, win, ws_pair, bias_full, gnm)
        osb = _sb_prompt(sb_bias[i], q, kb, vb, gns, seq)
        hp, conv_p = _ffn_prompt(hp, osb, oml, p_prompt[i].reshape(bp * seq, PLE_DIM), weights, seq, final)
        kp_l.append(k); vp_l.append(v); cp_l.append(conv_p)

        wrow = jnp.repeat(ws[:, :n_tok, :n_tok].transpose(1, 2, 0).reshape(n_tok * n_tok, N_GROUPS),
                          GROUP_DIM, axis=1)
        brow = jnp.repeat(b_spatial[i][:, :n_tok].T, GROUP_DIM, axis=1)
        q, k, v, vm, oml = _proj_sample(hs, gmix, win, wrow, brow, gnm, n_tok, n_seq)
        pad_rows = lambda x: jnp.pad(to_seq_major(x, n_tok), ((0, 0), (0, SUBLANES - n_tok), (0, 0)))
        bias_rows = jnp.broadcast_to(jnp.tile(sb_bias[i], n_tok)[:, None], (n_tok * N_HEADS, LANES))
        osb = _sb_sample(page_table, bias_rows, to_seq_major(q, n_tok), pad_rows(k), pad_rows(v), gns,
                         ckt, cvt, i)
        osb = to_tok_major(osb).astype(BF16)
        hs, conv_s = _ffn_sample(hs, osb, oml, to_tok_major(p_sample[i]), to_tok_major(state_conv[i]),
                                 weights, n_tok, n_seq, final)
        ks_l.append(to_seq_major(k, n_tok)); vs_l.append(to_seq_major(v, n_tok))
        cs_l.append(to_seq_major(conv_s, CONV_W - 1)); ms_l.append(to_seq_major(vm, n_tok))

    heads = lambda x: x.reshape(x.shape[:-1] + (N_HEADS, HEAD_DIM))
    return (hp.reshape(bp, seq, D_MODEL),
            to_seq_major(hs, n_tok),
            heads(jnp.stack(kp_l).reshape(depth, bp, seq, D_SB)),
            heads(jnp.stack(vp_l).reshape(depth, bp, seq, D_SB)),
            heads(jnp.stack(ks_l)),
            heads(jnp.stack(vs_l)),
            jnp.stack(cp_l),
            jnp.stack(cs_l),
            jnp.stack(ms_l))
```

```python
import functools
import math

import jax
import jax.numpy as jnp
from jax import lax
from jax.experimental import pallas as pl
from jax.experimental.pallas import tpu as pltpu

F32 = jnp.float32
BF16 = jnp.bfloat16

D_MODEL = 1024
N_HEADS = 8
HEAD_DIM = 64
D_SB = N_HEADS * HEAD_DIM
N_GROUPS = 8
GROUP_DIM = 64
D_MLP = N_GROUPS * GROUP_DIM
D_IN = 3 * D_SB + 2 * D_MLP
CHUNK = 128
D_FF = 2816
CONV_W = 3
PLE_DIM = 256
PAGE_SIZE = 128
EPS = 1e-6

LANES = 128
SUBLANES = 8
HEAD_PAIRS = D_SB // LANES
FF_CHUNK = 256
N_FF_CHUNKS = D_FF // FF_CHUNK
TM = 512
TQ = 256
VMEM_LIMIT = 56 * 1024 * 1024


def _bdot(a, b):
    return jnp.dot(a, b, preferred_element_type=F32)


def _rms(x, g):
    ms = jnp.mean(x * x, axis=-1, keepdims=True)
    return x * lax.rsqrt(ms + EPS) * g


def _gelu(x):
    c = math.sqrt(2.0 / math.pi)
    return x * (0.5 * (1.0 + jnp.tanh(c * (x + 0.044715 * (x * x * x)))))


def _resident(shape):
    nd = len(shape)
    return pl.BlockSpec(shape, lambda *_: (0,) * nd, pipeline_mode=pl.Buffered(1))


def _project(h_ref, g_ref, win_ref):
    hn = _rms(h_ref[...], g_ref[...]).astype(BF16)
    seg = lambda j: _bdot(hn, win_ref[:, j * D_SB:(j + 1) * D_SB])
    return seg(0) * (HEAD_DIM ** -0.5), seg(1), seg(2), _gelu(seg(3)), _gelu(seg(4))


def _proj_prompt_kernel(h_ref, g_ref, win_ref, wkvt_ref, ws_ref, bs_ref, gnm_ref, *refs, n_prev):
    prev = refs[:2] if n_prev else None
    q_ref, kb_ref, vtb_ref, om_ref, kt_ref, vt_ref = refs[2 if n_prev else 0:]
    hn = _rms(h_ref[...], g_ref[...]).astype(BF16)
    seg = lambda j: _bdot(hn, win_ref[:, j * D_SB:(j + 1) * D_SB])
    q_ref[...] = (seg(0) * (HEAD_DIM ** -0.5)).astype(BF16)
    kb_ref[...] = seg(1).astype(BF16)
    u, vm = _gelu(seg(3)), _gelu(seg(4))
    kt = _dot_nt(wkvt_ref[0:D_SB, :], hn)
    vt = _dot_nt(wkvt_ref[D_SB:2 * D_SB, :], hn)
    if n_prev:
        kt_ref[0:n_prev] = prev[0][...]
        vt_ref[0:n_prev] = prev[1][...]
    kt_ref[n_prev] = kt
    vt_ref[n_prev] = vt
    for c in range(vtb_ref.shape[0]):
        vtb_ref[c] = vt[:, c * TQ:(c + 1) * TQ].astype(BF16)

    row = lax.broadcasted_iota(jnp.int32, (CHUNK, 2 * CHUNK), 0)
    col = lax.broadcasted_iota(jnp.int32, (CHUNK, 2 * CHUNK), 1)
    tril = (col & (CHUNK - 1)) <= row
    wpair = [jnp.where(tril, ws_ref[p], 0.0).astype(BF16) for p in range(HEAD_PAIRS)]
    lo_half = lax.broadcasted_iota(jnp.int32, (CHUNK, LANES), 1) < GROUP_DIM
    for c in range(h_ref.shape[0] // CHUNK):
        rows = slice(c * CHUNK, (c + 1) * CHUNK)
        parts = []
        for p in range(HEAD_PAIRS):
            s = vm[rows, p * LANES:(p + 1) * LANES]
            rhs = jnp.concatenate([jnp.where(lo_half, s, 0.0).astype(BF16),
                                   jnp.where(lo_half, 0.0, s).astype(BF16)], axis=0)
            parts.append(_bdot(wpair[p], rhs))
        mixed = jnp.concatenate(parts, axis=1) + bs_ref[...]
        om_ref[rows, :] = _rms(u[rows] * mixed, gnm_ref[...]).astype(BF16)


def _proj_sample_kernel(h_ref, g_ref, win_ref, wrow_ref, brow_ref, gnm_ref,
                        q_ref, k_ref, v_ref, vm_ref, om_ref, *, n_tok, n_seq):
    q, k, v, u, vm = _project(h_ref, g_ref, win_ref)
    q_ref[...] = q
    k_ref[...] = k
    v_ref[...] = v
    vm_ref[...] = vm
    for t in range(n_tok):
        mixed = brow_ref[t:t + 1, :]
        for s in range(t + 1):
            w = wrow_ref[t * n_tok + s:t * n_tok + s + 1, :]
            mixed = mixed + w * vm[s * n_seq:(s + 1) * n_seq]
        rows = slice(t * n_seq, (t + 1) * n_seq)
        om_ref[rows, :] = _rms(u[rows] * mixed, gnm_ref[...]).astype(BF16)


def _dot_nt(a, b):
    return lax.dot_general(a, b, (((1,), (1,)), ((), ())), preferred_element_type=F32)


def _sb_suffix(z, ucat, mask, key_axis=1):
    lg = jnp.log(1.0 + jnp.exp(-jnp.abs(z)))
    log_b = jnp.minimum(z, 0.0) - lg
    log_1mb = log_b - z
    if mask is not None:
        log_1mb = jnp.where(mask, log_1mb, 0.0)
    hi = log_1mb.astype(BF16)
    lo = (log_1mb - hi.astype(F32)).astype(BF16)
    split = jnp.concatenate([hi, lo], axis=key_axis)
    suffix = _bdot(split, ucat) if key_axis == 1 else _bdot(ucat, split)
    first = (slice(None), slice(0, 1)) if key_axis == 1 else (slice(0, 1), slice(None))
    return log_b, suffix, suffix[first] + log_1mb[first]


def _sb_weights(z, carry, ucat, mask):
    log_b, suffix, total = _sb_suffix(z, ucat, mask)
    a = jnp.exp(log_b + suffix + carry)
    if mask is not None:
        a = jnp.where(mask, a, 0.0)
    return a, carry + total


def _suffix_ones(tk, key_axis=1):
    r = lax.broadcasted_iota(jnp.int32, (tk, tk), 0)
    c = lax.broadcasted_iota(jnp.int32, (tk, tk), 1)
    u = jnp.where(r > c if key_axis == 1 else c > r, 1.0, 0.0).astype(BF16)
    return jnp.concatenate([u, u], axis=1 - key_axis)


def _sb_prompt_kernel(bias_ref, q_ref, k_ref, vt_ref, gn_ref, o_ref, qmt_ref, acc_ref):
    tq = q_ref.shape[0]
    qi = pl.program_id(1)
    ucat = _suffix_ones(tq, key_axis=0)
    r = lax.broadcasted_iota(jnp.int32, (tq, 2 * tq), 0)
    c = lax.broadcasted_iota(jnp.int32, (tq, 2 * tq), 1)
    causal = r < (c & (tq - 1))
    lo_half = lax.broadcasted_iota(jnp.int32, (LANES, tq), 0) < HEAD_DIM
    pair = [slice(p * LANES, (p + 1) * LANES) for p in range(HEAD_PAIRS)]

    for p in range(HEAD_PAIRS):
        q2t = q_ref[:, pair[p]].astype(F32).T
        qmt_ref[p] = jnp.concatenate([jnp.where(lo_half, q2t, 0.0), jnp.where(lo_half, 0.0, q2t)],
                                     axis=1).astype(BF16)

    def key_block(j, carries, mask, first):
        start = pl.multiple_of(j * tq, tq)
        logits, terms, out = [], [], []
        for p in range(HEAD_PAIRS):
            s = _bdot(k_ref[pl.ds(start, tq), pair[p]], qmt_ref[p])
            logits.append(jnp.concatenate([s[:, :tq] + bias_ref[2 * p], s[:, tq:] + bias_ref[2 * p + 1]],
                                          axis=1))
        for p in range(HEAD_PAIRS):
            terms.append(_sb_suffix(logits[p], ucat, mask, key_axis=0))
        for p in range(HEAD_PAIRS):
            log_b, suffix, total = terms[p]
            a = jnp.exp(log_b + suffix + carries[p])
            if mask is not None:
                a = jnp.where(mask, a, 0.0)
            o = _bdot(vt_ref[j, pair[p], :], a.astype(BF16))
            o = jnp.where(lo_half, o[:, :tq], o[:, tq:])
            acc_ref[pair[p], :] = o if first else acc_ref[pair[p], :] + o
            out.append(carries[p] + total)
        return tuple(out)

    carries = key_block(qi, (jnp.zeros((1, 2 * tq), F32),) * HEAD_PAIRS, causal, True)
    lax.fori_loop(0, qi, lambda i, c: key_block(qi - 1 - i, c, None, False), carries)
    o_ref[...] = _rms(acc_ref[...].T, gn_ref[...]).astype(BF16)


def _sb_sample_kernel(pt_ref, bias_ref, q_ref, kn_ref, vn_ref, gn_ref, *refs, n_tok, n_pages):
    del pt_ref
    kpages = refs[:n_pages]
    vpages = refs[n_pages:2 * n_pages]
    o_ref, kt_ref, vt_ref = refs[2 * n_pages:]
    m = n_tok * N_HEADS
    tk = 2 * PAGE_SIZE
    past = n_pages * PAGE_SIZE
    head_of_lane = lax.broadcasted_iota(jnp.int32, (N_HEADS, D_SB), 1) // HEAD_DIM
    own = head_of_lane == lax.broadcasted_iota(jnp.int32, (N_HEADS, D_SB), 0)
    q = q_ref[...]
    qbd = jnp.concatenate(
        [jnp.where(own, jnp.broadcast_to(q[t:t + 1, :], (N_HEADS, D_SB)), 0.0) for t in range(n_tok)],
        axis=0).astype(BF16)
    bias = bias_ref[:, 0:1]

    for j in range(n_pages):
        kt_ref[:, j * PAGE_SIZE:(j + 1) * PAGE_SIZE] = kpages[j][...].astype(BF16)
        vt_ref[:, j * PAGE_SIZE:(j + 1) * PAGE_SIZE] = vpages[j][...].astype(BF16)

    pad = jnp.zeros((PAGE_SIZE - SUBLANES, D_SB), F32)
    kn = jnp.concatenate([kn_ref[...], pad], axis=0).astype(BF16)
    vn = jnp.concatenate([vn_ref[...], pad], axis=0).astype(BF16)
    tok = lax.broadcasted_iota(jnp.int32, (m, PAGE_SIZE), 0) // N_HEADS
    mask = lax.broadcasted_iota(jnp.int32, (m, PAGE_SIZE), 1) < tok
    a, carry = _sb_weights(_dot_nt(qbd, kn) + bias, jnp.zeros((m, 1), F32), _suffix_ones(PAGE_SIZE), mask)
    acc = _bdot(a.astype(BF16), vn)

    z = _bdot(qbd, kt_ref[...]) + bias
    ucat = _suffix_ones(tk)
    blocks = [None] * (past // tk)
    for j in reversed(range(past // tk)):
        a, carry = _sb_weights(z[:, j * tk:(j + 1) * tk], carry, ucat, None)
        blocks[j] = a.astype(BF16)
    acc = acc + _dot_nt(jnp.concatenate(blocks, axis=1), vt_ref[...])

    for t in range(n_tok):
        blk = jnp.where(own, acc[t * N_HEADS:(t + 1) * N_HEADS, :], 0.0)
        row = jnp.sum(blk, axis=0, keepdims=True)
        o_ref[t:t + 1, :] = _rms(row, gn_ref[...])


def _mix_in(h_ref, osb_ref, oml_ref, wout_ref, gffn_ref):
    o = jnp.concatenate([osb_ref[...], oml_ref[...]], axis=1)
    h1 = h_ref[...] + _bdot(o, wout_ref[...])
    return h1, _rms(h1, gffn_ref[...]).astype(BF16)


def _ffn_chunk(hn, wup_ref, c):
    cols = slice(c * FF_CHUNK, (c + 1) * FF_CHUNK)
    gate = _bdot(hn, wup_ref[:, cols])
    up = _bdot(hn, wup_ref[:, D_FF + c * FF_CHUNK:D_FF + (c + 1) * FF_CHUNK])
    return cols, gate, up


def _ple_out(h1, act_ref, wdn_ref, p_ref, gple_ref, wg_ref, wp_ref, gfin_ref, out_ref, final):
    h2 = h1 + _bdot(act_ref[...], wdn_ref[...])
    gate = jax.nn.sigmoid(_bdot(_rms(h2, gple_ref[...]).astype(BF16), wg_ref[...]))
    h3 = h2 + gate * _bdot(p_ref[...].astype(BF16), wp_ref[...])
    out_ref[...] = _rms(h3, gfin_ref[...]) if final else h3


def _ffn_prompt_kernel(h_ref, osb_ref, oml_ref, p_ref, wout_ref, gffn_ref, wup_ref, cw_ref, cb_ref,
                       wdn_ref, gple_ref, wg_ref, wp_ref, gfin_ref,
                       out_ref, cst_ref, act_ref, carry_ref, *, tiles_per_seq, final):
    tm = h_ref.shape[0]
    h1, hn = _mix_in(h_ref, osb_ref, oml_ref, wout_ref, gffn_ref)

    @pl.when(pl.program_id(0) % tiles_per_seq == 0)
    def _():
        carry_ref[...] = jnp.zeros_like(carry_ref)

    rowi = lax.broadcasted_iota(jnp.int32, (SUBLANES, FF_CHUNK), 0)
    for c in range(N_FF_CHUNKS):
        cols, gate, up = _ffn_chunk(hn, wup_ref, c)
        prev = carry_ref[:, cols]
        carry_ref[:, cols] = gate[tm - SUBLANES:tm, :]
        cst_ref[:, cols] = gate[tm - (CONV_W - 1):tm, :]
        conv = cb_ref[:, cols] + cw_ref[CONV_W - 1:CONV_W, cols] * gate
        for d in range(1, CONV_W):
            rolled = pltpu.roll(gate, d, axis=0)
            top = jnp.where(rowi < d, pltpu.roll(prev, d, axis=0), rolled[0:SUBLANES])
            shifted = jnp.concatenate([top, rolled[SUBLANES:]], axis=0)
            conv = conv + cw_ref[CONV_W - 1 - d:CONV_W - d, cols] * shifted
        act_ref[:, cols] = (_gelu(conv) * up).astype(BF16)
    _ple_out(h1, act_ref, wdn_ref, p_ref, gple_ref, wg_ref, wp_ref, gfin_ref, out_ref, final)


def _ffn_sample_kernel(h_ref, osb_ref, oml_ref, p_ref, pre_ref, wout_ref, gffn_ref, wup_ref, cw_ref, cb_ref,
                       wdn_ref, gple_ref, wg_ref, wp_ref, gfin_ref,
                       out_ref, cst_ref, act_ref, *, n_tok, n_seq, final):
    h1, hn = _mix_in(h_ref, osb_ref, oml_ref, wout_ref, gffn_ref)
    n_pre = CONV_W - 1
    for c in range(N_FF_CHUNKS):
        cols, gate, up = _ffn_chunk(hn, wup_ref, c)
        gp = [pre_ref[j * n_seq:(j + 1) * n_seq, cols] for j in range(n_pre)]
        gp += [gate[t * n_seq:(t + 1) * n_seq] for t in range(n_tok)]
        for j in range(n_pre):
            cst_ref[j * n_seq:(j + 1) * n_seq, cols] = gp[n_tok + j]
        conv = []
        for t in range(n_tok):
            acc = cb_ref[:, cols]
            for j in range(CONV_W):
                acc = acc + cw_ref[j:j + 1, cols] * gp[t + j]
            conv.append(acc)
        act_ref[:, cols] = (_gelu(jnp.concatenate(conv, axis=0)) * up).astype(BF16)
    _ple_out(h1, act_ref, wdn_ref, p_ref, gple_ref, wg_ref, wp_ref, gfin_ref, out_ref, final)


def _params(*sem):
    return pltpu.CompilerParams(dimension_semantics=sem, vmem_limit_bytes=VMEM_LIMIT)


def _proj_prompt(h, g, win, wkvt, ws_pair, bias_full, gnm, seq, prev):
    n = h.shape[0]
    tps = seq // TM
    n_prev = 0 if prev is None else prev[0].shape[0]
    tile = lambda w: pl.BlockSpec((TM, w), lambda i: (i, 0))
    stack = lambda layers: pl.BlockSpec((layers, None, D_SB, TM), lambda i: (0, i // tps, 0, i % tps))
    kvt_shape = jax.ShapeDtypeStruct((n_prev + 1, n // seq, D_SB, seq), F32)
    return pl.pallas_call(
        functools.partial(_proj_prompt_kernel, n_prev=n_prev),
        grid=(n // TM,),
        in_specs=[tile(D_MODEL), _resident((1, D_MODEL)), _resident((D_MODEL, D_IN)),
                  _resident((2 * D_SB, D_MODEL)), _resident(ws_pair.shape), _resident(bias_full.shape),
                  _resident((1, D_MLP))] + [stack(n_prev)] * (2 if n_prev else 0),
        out_specs=[tile(D_SB), tile(D_SB), pl.BlockSpec((TM // TQ, D_SB, TQ), lambda i: (i, 0, 0)),
                   tile(D_MLP), stack(n_prev + 1), stack(n_prev + 1)],
        out_shape=[jax.ShapeDtypeStruct((n, D_SB), BF16), jax.ShapeDtypeStruct((n, D_SB), BF16),
                   jax.ShapeDtypeStruct((n // TQ, D_SB, TQ), BF16), jax.ShapeDtypeStruct((n, D_MLP), BF16),
                   kvt_shape, kvt_shape],
        compiler_params=_params("arbitrary"),
        name="proj_prompt",
    )(h, g, win, wkvt, ws_pair, bias_full, gnm, *(prev or ()))


def _proj_sample(h, g, win, wrow, brow, gnm, n_tok, n_seq):
    n = h.shape[0]
    full = lambda w: pl.BlockSpec((n, w), lambda i: (0, 0))
    return pl.pallas_call(
        functools.partial(_proj_sample_kernel, n_tok=n_tok, n_seq=n_seq),
        grid=(1,),
        in_specs=[full(D_MODEL), _resident((1, D_MODEL)), _resident((D_MODEL, D_IN)),
                  _resident(wrow.shape), _resident(brow.shape), _resident((1, D_MLP))],
        out_specs=[full(D_SB)] * 5,
        out_shape=[jax.ShapeDtypeStruct((n, D_SB), F32)] * 4 + [jax.ShapeDtypeStruct((n, D_MLP), BF16)],
        compiler_params=_params("arbitrary"),
        name="proj_sample",
    )(h, g, win, wrow, brow, gnm)


def _sb_prompt(bias, q, kb, vtb, gn, seq):
    n = q.shape[0]
    nq = seq // TQ
    return pl.pallas_call(
        _sb_prompt_kernel,
        grid=(n // seq, nq),
        in_specs=[pl.BlockSpec(memory_space=pltpu.SMEM),
                  pl.BlockSpec((TQ, D_SB), lambda b, i: (b * nq + i, 0)),
                  pl.BlockSpec((seq, D_SB), lambda b, i: (b, 0)),
                  pl.BlockSpec((nq, D_SB, TQ), lambda b, i: (b, 0, 0)),
                  _resident((1, D_SB))],
        out_specs=pl.BlockSpec((TQ, D_SB), lambda b, i: (b * nq + i, 0)),
        out_shape=jax.ShapeDtypeStruct((n, D_SB), BF16),
        scratch_shapes=[pltpu.VMEM((HEAD_PAIRS, LANES, 2 * TQ), BF16), pltpu.VMEM((D_SB, TQ), F32)],
        compiler_params=_params("arbitrary", "arbitrary"),
        name="sb_prompt",
    )(bias, q, kb, vtb, gn)


def _sb_sample(page_table, bias_rows, q, k_new, v_new, gn, cache_kt, cache_vt, layer):
    n_seq, n_pages = page_table.shape
    n_tok = q.shape[1]
    page = lambda j: pl.BlockSpec((None, None, D_SB, PAGE_SIZE), lambda b, pt, j=j: (layer, pt[b, j], 0, 0))
    per_seq = lambda rows: pl.BlockSpec((None, rows, D_SB), lambda b, pt: (b, 0, 0))
    const = lambda shape: pl.BlockSpec(shape, lambda b, pt: (0,) * len(shape), pipeline_mode=pl.Buffered(1))
    grid_spec = pltpu.PrefetchScalarGridSpec(
        num_scalar_prefetch=1,
        grid=(n_seq,),
        in_specs=[const(bias_rows.shape), per_seq(n_tok), per_seq(SUBLANES), per_seq(SUBLANES),
                  const((1, D_SB))] + [page(j) for j in range(n_pages)] * 2,
        out_specs=per_seq(n_tok),
        scratch_shapes=[pltpu.VMEM((D_SB, n_pages * PAGE_SIZE), BF16)] * 2,
    )
    return pl.pallas_call(
        functools.partial(_sb_sample_kernel, n_tok=n_tok, n_pages=n_pages),
        grid_spec=grid_spec,
        out_shape=jax.ShapeDtypeStruct((n_seq, n_tok, D_SB), F32),
        compiler_params=_params("arbitrary"),
        name="sb_sample",
    )(page_table, bias_rows, q, k_new, v_new, gn, *([cache_kt] * n_pages), *([cache_vt] * n_pages))


def _weight_specs():
    return [_resident((D_MODEL, D_MODEL)), _resident((1, D_MODEL)), _resident((D_MODEL, 2 * D_FF)),
            _resident((CONV_W, D_FF)), _resident((1, D_FF)), _resident((D_FF, D_MODEL)),
            _resident((1, D_MODEL)), _resident((D_MODEL, D_MODEL)), _resident((PLE_DIM, D_MODEL)),
            _resident((1, D_MODEL))]


def _ffn_prompt(h, osb, oml, p, weights, seq, final):
    n = h.shape[0]
    tile = lambda w: pl.BlockSpec((TM, w), lambda i: (i, 0))
    tiles_per_seq = seq // TM
    return pl.pallas_call(
        functools.partial(_ffn_prompt_kernel, tiles_per_seq=tiles_per_seq, final=final),
        grid=(n // TM,),
        in_specs=[tile(D_MODEL), tile(D_SB), tile(D_MLP), tile(PLE_DIM)] + _weight_specs(),
        out_specs=[tile(D_MODEL),
                   pl.BlockSpec((None, CONV_W - 1, D_FF), lambda i: (i // tiles_per_seq, 0, 0))],
        out_shape=[jax.ShapeDtypeStruct((n, D_MODEL), F32),
                   jax.ShapeDtypeStruct((n // seq, CONV_W - 1, D_FF), F32)],
        scratch_shapes=[pltpu.VMEM((TM, D_FF), BF16), pltpu.VMEM((SUBLANES, D_FF), F32)],
        compiler_params=_params("arbitrary"),
        name="ffn_prompt",
    )(h, osb, oml, p, *weights)


def _ffn_sample(h, osb, oml, p, prefix, weights, n_tok, n_seq, final):
    n = h.shape[0]
    full = lambda rows, w: pl.BlockSpec((rows, w), lambda i: (0, 0))
    n_pre = (CONV_W - 1) * n_seq
    return pl.pallas_call(
        functools.partial(_ffn_sample_kernel, n_tok=n_tok, n_seq=n_seq, final=final),
        grid=(1,),
        in_specs=[full(n, D_MODEL), full(n, D_SB), full(n, D_MLP), full(n, PLE_DIM),
                  full(n_pre, D_FF)] + _weight_specs(),
        out_specs=[full(n, D_MODEL), full(n_pre, D_FF)],
        out_shape=[jax.ShapeDtypeStruct((n, D_MODEL), F32), jax.ShapeDtypeStruct((n_pre, D_FF), F32)],
        scratch_shapes=[pltpu.VMEM((n, D_FF), BF16)],
        compiler_params=_params("arbitrary"),
        name="ffn_sample",
    )(h, osb, oml, p, prefix, *weights)


def kernel(x_prompt, x_sample, cache_k, cache_v, state_conv, page_table, p_prompt, p_sample, norm_mix_g, w_in, sb_bias, gn_sb, gn_mlp, w_spatial, b_spatial, w_out, norm_ffn_g, w_up, conv_w, conv_b, w_down, norm_ple_g, w_ple_gate, w_ple_proj, final_norm_g):
    depth = w_in.shape[0]
    bp, seq, _ = x_prompt.shape
    n_seq, n_tok, _ = x_sample.shape
    n_phys = cache_k.shape[1]

    to_tok_major = lambda x: jnp.swapaxes(x, 0, 1).reshape((x.shape[0] * x.shape[1],) + x.shape[2:])
    to_seq_major = lambda x, t: jnp.swapaxes(x.reshape((t, n_seq) + x.shape[1:]), 0, 1)
    row = lambda x: x.reshape(1, -1)

    hp = x_prompt.reshape(bp * seq, D_MODEL)
    hs = to_tok_major(x_sample)
    page_t = lambda c: jnp.transpose(c, (0, 1, 3, 4, 2)).reshape(depth, n_phys, D_SB, PAGE_SIZE)
    ckt, cvt = page_t(cache_k), page_t(cache_v)
    gfin = row(final_norm_g)

    ks_l, vs_l, cp_l, cs_l, ms_l = [], [], [], [], []
    kvt = None
    for i in range(depth):
        final = i == depth - 1
        win = w_in[i].astype(BF16)
        weights = (w_out[i].astype(BF16), row(norm_ffn_g[i]), w_up[i].astype(BF16), conv_w[i], row(conv_b[i]),
                   w_down[i].astype(BF16), row(norm_ple_g[i]), w_ple_gate[i].astype(BF16),
                   w_ple_proj[i].astype(BF16), gfin)
        gmix, gnm, gns = row(norm_mix_g[i]), row(gn_mlp[i]), row(gn_sb[i])

        ws = w_spatial[i]
        ws_pair = jnp.concatenate([ws[0::2], ws[1::2]], axis=2)
        bias_full = jnp.repeat(b_spatial[i].T, GROUP_DIM, axis=1)
        wkvt = w_in[i][:, D_SB:3 * D_SB].T.astype(BF16)
        q, kb, vtb, oml, kt_all, vt_all = _proj_prompt(hp, gmix, win, wkvt, ws_pair, bias_full, gnm, seq, kvt)
        kvt = (kt_all, vt_all)
        osb = _sb_prompt(sb_bias[i], q, kb, vtb, gns, seq)
        hp, conv_p = _ffn_prompt(hp, osb, oml, p_prompt[i].reshape(bp * seq, PLE_DIM), weights, seq, final)
        cp_l.append(conv_p)

        wrow = jnp.repeat(ws[:, :n_tok, :n_tok].transpose(1, 2, 0).reshape(n_tok * n_tok, N_GROUPS),
                          GROUP_DIM, axis=1)
        brow = jnp.repeat(b_spatial[i][:, :n_tok].T, GROUP_DIM, axis=1)
        q, k, v, vm, oml = _proj_sample(hs, gmix, win, wrow, brow, gnm, n_tok, n_seq)
        pad_rows = lambda x: jnp.pad(to_seq_major(x, n_tok), ((0, 0), (0, SUBLANES - n_tok), (0, 0)))
        bias_rows = jnp.broadcast_to(jnp.tile(sb_bias[i], n_tok)[:, None], (n_tok * N_HEADS, LANES))
        osb = _sb_sample(page_table, bias_rows, to_seq_major(q, n_tok), pad_rows(k), pad_rows(v), gns,
                         ckt, cvt, i)
        osb = to_tok_major(osb).astype(BF16)
        hs, conv_s = _ffn_sample(hs, osb, oml, to_tok_major(p_sample[i]), to_tok_major(state_conv[i]),
                                 weights, n_tok, n_seq, final)
        ks_l.append(to_seq_major(k, n_tok)); vs_l.append(to_seq_major(v, n_tok))
        cs_l.append(to_seq_major(conv_s, CONV_W - 1)); ms_l.append(to_seq_major(vm, n_tok))

    heads = lambda x: x.reshape(x.shape[:-1] + (N_HEADS, HEAD_DIM))
    from_t = lambda x: jnp.transpose(x.reshape(depth, bp, N_HEADS, HEAD_DIM, seq), (0, 1, 4, 2, 3))
    return (hp.reshape(bp, seq, D_MODEL),
            to_seq_major(hs, n_tok),
            from_t(kvt[0]),
            from_t(kvt[1]),
            heads(jnp.stack(ks_l)),
            heads(jnp.stack(vs_l)),
            jnp.stack(cp_l),
            jnp.stack(cs_l),
            jnp.stack(ms_l))
```

```python
import functools
import math

import jax
import jax.numpy as jnp
from jax import lax
from jax.experimental import pallas as pl
from jax.experimental.pallas import tpu as pltpu

F32 = jnp.float32
BF16 = jnp.bfloat16

D_MODEL = 1024
N_HEADS = 8
HEAD_DIM = 64
D_SB = N_HEADS * HEAD_DIM
N_GROUPS = 8
GROUP_DIM = 64
D_MLP = N_GROUPS * GROUP_DIM
D_IN = 3 * D_SB + 2 * D_MLP
CHUNK = 128
D_FF = 2816
CONV_W = 3
PLE_DIM = 256
PAGE_SIZE = 128
EPS = 1e-6

LANES = 128
SUBLANES = 8
HEAD_PAIRS = D_SB // LANES
FF_CHUNK = 256
N_FF_CHUNKS = D_FF // FF_CHUNK
TM = 512
TQ = 256
VMEM_LIMIT = 56 * 1024 * 1024


def _bdot(a, b):
    return jnp.dot(a, b, preferred_element_type=F32)


def _rms(x, g):
    ms = jnp.mean(x * x, axis=-1, keepdims=True)
    return x * lax.rsqrt(ms + EPS) * g


def _gelu(x):
    c = math.sqrt(2.0 / math.pi)
    return x * (0.5 * (1.0 + jnp.tanh(c * (x + 0.044715 * (x * x * x)))))


def _resident(shape):
    nd = len(shape)
    return pl.BlockSpec(shape, lambda *_: (0,) * nd, pipeline_mode=pl.Buffered(1))


def _of_layer(shape, layer):
    nd = len(shape)
    return pl.BlockSpec((None,) + tuple(shape), lambda *_: (layer,) + (0,) * nd, pipeline_mode=pl.Buffered(1))


def _project(h_ref, g_ref, win_ref):
    hn = _rms(h_ref[...], g_ref[...]).astype(BF16)
    seg = lambda j: _bdot(hn, win_ref[:, j * D_SB:(j + 1) * D_SB])
    return seg(0) * (HEAD_DIM ** -0.5), seg(1), seg(2), _gelu(seg(3)), _gelu(seg(4))


def _proj_prompt_kernel(h_ref, g_ref, win_ref, wkvt_ref, ws_ref, bs_ref, gnm_ref, *refs, n_prev):
    prev = refs[:2] if n_prev else None
    q_ref, kb_ref, vtb_ref, om_ref, kt_ref, vt_ref = refs[2 if n_prev else 0:]
    hn = _rms(h_ref[...], g_ref[...]).astype(BF16)
    seg = lambda j: _bdot(hn, win_ref[:, j * D_SB:(j + 1) * D_SB])
    q_ref[...] = (seg(0) * (HEAD_DIM ** -0.5)).astype(BF16)
    kb_ref[...] = seg(1).astype(BF16)
    u, vm = _gelu(seg(3)), _gelu(seg(4))
    kt = _dot_nt(wkvt_ref[0:D_SB, :], hn)
    vt = _dot_nt(wkvt_ref[D_SB:2 * D_SB, :], hn)
    if n_prev:
        kt_ref[0:n_prev] = prev[0][...]
        vt_ref[0:n_prev] = prev[1][...]
    kt_ref[n_prev] = kt
    vt_ref[n_prev] = vt
    for c in range(vtb_ref.shape[0]):
        vtb_ref[c] = vt[:, c * TQ:(c + 1) * TQ].astype(BF16)

    row = lax.broadcasted_iota(jnp.int32, (CHUNK, 2 * CHUNK), 0)
    col = lax.broadcasted_iota(jnp.int32, (CHUNK, 2 * CHUNK), 1)
    tril = (col & (CHUNK - 1)) <= row
    wpair = [jnp.where(tril, ws_ref[p], 0.0).astype(BF16) for p in range(HEAD_PAIRS)]
    lo_half = lax.broadcasted_iota(jnp.int32, (CHUNK, LANES), 1) < GROUP_DIM
    for c in range(h_ref.shape[0] // CHUNK):
        rows = slice(c * CHUNK, (c + 1) * CHUNK)
        parts = []
        for p in range(HEAD_PAIRS):
            s = vm[rows, p * LANES:(p + 1) * LANES]
            rhs = jnp.concatenate([jnp.where(lo_half, s, 0.0).astype(BF16),
                                   jnp.where(lo_half, 0.0, s).astype(BF16)], axis=0)
            parts.append(_bdot(wpair[p], rhs))
        mixed = jnp.concatenate(parts, axis=1) + bs_ref[...]
        om_ref[rows, :] = _rms(u[rows] * mixed, gnm_ref[...]).astype(BF16)


def _proj_sample_kernel(h_ref, g_ref, win_ref, wrow_ref, brow_ref, gnm_ref,
                        q_ref, k_ref, v_ref, vm_ref, om_ref, *, n_tok, n_seq):
    q, k, v, u, vm = _project(h_ref, g_ref, win_ref)
    q_ref[...] = q
    k_ref[...] = k
    v_ref[...] = v
    vm_ref[...] = vm
    for t in range(n_tok):
        mixed = brow_ref[t:t + 1, :]
        for s in range(t + 1):
            w = wrow_ref[t * n_tok + s:t * n_tok + s + 1, :]
            mixed = mixed + w * vm[s * n_seq:(s + 1) * n_seq]
        rows = slice(t * n_seq, (t + 1) * n_seq)
        om_ref[rows, :] = _rms(u[rows] * mixed, gnm_ref[...]).astype(BF16)


def _dot_nt(a, b):
    return lax.dot_general(a, b, (((1,), (1,)), ((), ())), preferred_element_type=F32)


def _sb_suffix(z, ucat, mask, key_axis=1):
    lg = jnp.log(1.0 + jnp.exp(-jnp.abs(z)))
    log_b = jnp.minimum(z, 0.0) - lg
    log_1mb = log_b - z
    if mask is not None:
        log_1mb = jnp.where(mask, log_1mb, 0.0)
    hi = log_1mb.astype(BF16)
    lo = (log_1mb - hi.astype(F32)).astype(BF16)
    split = jnp.concatenate([hi, lo], axis=key_axis)
    suffix = _bdot(split, ucat) if key_axis == 1 else _bdot(ucat, split)
    first = (slice(None), slice(0, 1)) if key_axis == 1 else (slice(0, 1), slice(None))
    return log_b, suffix, suffix[first] + log_1mb[first]


def _sb_weights(z, carry, ucat, mask):
    log_b, suffix, total = _sb_suffix(z, ucat, mask)
    a = jnp.exp(log_b + suffix + carry)
    if mask is not None:
        a = jnp.where(mask, a, 0.0)
    return a, carry + total


def _suffix_ones(tk, key_axis=1):
    r = lax.broadcasted_iota(jnp.int32, (tk, tk), 0)
    c = lax.broadcasted_iota(jnp.int32, (tk, tk), 1)
    u = jnp.where(r > c if key_axis == 1 else c > r, 1.0, 0.0).astype(BF16)
    return jnp.concatenate([u, u], axis=1 - key_axis)


def _sb_prompt_kernel(bias_ref, q_ref, k_ref, vt_ref, gn_ref, o_ref, qmt_ref, acc_ref, *, layer):
    tq = q_ref.shape[0]
    qi = pl.program_id(1)
    ucat = _suffix_ones(tq, key_axis=0)
    r = lax.broadcasted_iota(jnp.int32, (tq, 2 * tq), 0)
    c = lax.broadcasted_iota(jnp.int32, (tq, 2 * tq), 1)
    causal = r < (c & (tq - 1))
    lo_half = lax.broadcasted_iota(jnp.int32, (LANES, tq), 0) < HEAD_DIM
    pair = [slice(p * LANES, (p + 1) * LANES) for p in range(HEAD_PAIRS)]

    for p in range(HEAD_PAIRS):
        q2t = q_ref[:, pair[p]].astype(F32).T
        qmt_ref[p] = jnp.concatenate([jnp.where(lo_half, q2t, 0.0), jnp.where(lo_half, 0.0, q2t)],
                                     axis=1).astype(BF16)

    def key_blocks(js, carries, mask, first):
        work = [(j, p) for j in js for p in range(HEAD_PAIRS)]
        logits, terms = [], []
        for j, p in work:
            start = pl.multiple_of(j * tq, tq)
            s = _bdot(k_ref[pl.ds(start, tq), pair[p]], qmt_ref[p])
            logits.append(jnp.concatenate([s[:, :tq] + bias_ref[layer, 2 * p],
                                           s[:, tq:] + bias_ref[layer, 2 * p + 1]], axis=1))
        for z in logits:
            terms.append(_sb_suffix(z, ucat, mask, key_axis=0))
        carries = list(carries)
        for n, (j, p) in enumerate(work):
            log_b, suffix, total = terms[n]
            a = jnp.exp(log_b + suffix)
            if mask is not None:
                a = jnp.where(mask, a, 0.0)
            o = _bdot(vt_ref[j, pair[p], :], a.astype(BF16)) * jnp.exp(carries[p])
            o = jnp.where(lo_half, o[:, :tq], o[:, tq:])
            acc_ref[pair[p], :] = o if first and n < HEAD_PAIRS else acc_ref[pair[p], :] + o
            carries[p] = carries[p] + total
        return tuple(carries)

    carries = key_blocks([qi], (jnp.zeros((1, 2 * tq), F32),) * HEAD_PAIRS, causal, True)
    carries = lax.fori_loop(0, lax.shift_right_logical(qi, 1),
                            lambda i, c: key_blocks([qi - 1 - 2 * i, qi - 2 - 2 * i], c, None, False), carries)
    lax.fori_loop(0, qi & 1, lambda i, c: key_blocks([0], c, None, False), carries)
    o_ref[...] = _rms(acc_ref[...].T, gn_ref[...]).astype(BF16)


def _sb_sample_kernel(pt_ref, bias_ref, q_ref, kn_ref, vn_ref, gn_ref, *refs, n_tok, n_pages):
    del pt_ref
    kpages = refs[:n_pages]
    vpages = refs[n_pages:2 * n_pages]
    o_ref, kt_ref, vt_ref = refs[2 * n_pages:]
    m = n_tok * N_HEADS
    tk = 2 * PAGE_SIZE
    past = n_pages * PAGE_SIZE
    head_of_lane = lax.broadcasted_iota(jnp.int32, (N_HEADS, D_SB), 1) // HEAD_DIM
    own = head_of_lane == lax.broadcasted_iota(jnp.int32, (N_HEADS, D_SB), 0)
    q = q_ref[...]
    qbd = jnp.concatenate(
        [jnp.where(own, jnp.broadcast_to(q[t:t + 1, :], (N_HEADS, D_SB)), 0.0) for t in range(n_tok)],
        axis=0).astype(BF16)
    bias = bias_ref[:, 0:1]

    for j in range(n_pages):
        kt_ref[:, j * PAGE_SIZE:(j + 1) * PAGE_SIZE] = kpages[j][...].astype(BF16)
        vt_ref[:, j * PAGE_SIZE:(j + 1) * PAGE_SIZE] = vpages[j][...].astype(BF16)

    pad = jnp.zeros((PAGE_SIZE - SUBLANES, D_SB), F32)
    kn = jnp.concatenate([kn_ref[...], pad], axis=0).astype(BF16)
    vn = jnp.concatenate([vn_ref[...], pad], axis=0).astype(BF16)
    tok = lax.broadcasted_iota(jnp.int32, (m, PAGE_SIZE), 0) // N_HEADS
    mask = lax.broadcasted_iota(jnp.int32, (m, PAGE_SIZE), 1) < tok
    a, carry = _sb_weights(_dot_nt(qbd, kn) + bias, jnp.zeros((m, 1), F32), _suffix_ones(PAGE_SIZE), mask)
    acc = _bdot(a.astype(BF16), vn)

    z = _bdot(qbd, kt_ref[...]) + bias
    ucat = _suffix_ones(tk)
    blocks = [None] * (past // tk)
    for j in reversed(range(past // tk)):
        a, carry = _sb_weights(z[:, j * tk:(j + 1) * tk], carry, ucat, None)
        blocks[j] = a.astype(BF16)
    acc = acc + _dot_nt(jnp.concatenate(blocks, axis=1), vt_ref[...])

    for t in range(n_tok):
        blk = jnp.where(own, acc[t * N_HEADS:(t + 1) * N_HEADS, :], 0.0)
        row = jnp.sum(blk, axis=0, keepdims=True)
        o_ref[t:t + 1, :] = _rms(row, gn_ref[...])


def _mix_in(h_ref, osb_ref, oml_ref, wout_ref, gffn_ref):
    o = jnp.concatenate([osb_ref[...], oml_ref[...]], axis=1)
    h1 = h_ref[...] + _bdot(o, wout_ref[...])
    return h1, _rms(h1, gffn_ref[...]).astype(BF16)


def _ffn_chunk(hn, wup_ref, c):
    cols = slice(c * FF_CHUNK, (c + 1) * FF_CHUNK)
    gate = _bdot(hn, wup_ref[:, cols])
    up = _bdot(hn, wup_ref[:, D_FF + c * FF_CHUNK:D_FF + (c + 1) * FF_CHUNK])
    return cols, gate, up


def _ple_out(h1, act_ref, wdn_ref, p_ref, gple_ref, wg_ref, wp_ref, gfin_ref, out_ref, final):
    h2 = h1 + _bdot(act_ref[...], wdn_ref[...])
    gate = jax.nn.sigmoid(_bdot(_rms(h2, gple_ref[...]).astype(BF16), wg_ref[...]))
    h3 = h2 + gate * _bdot(p_ref[...].astype(BF16), wp_ref[...])
    out_ref[...] = _rms(h3, gfin_ref[...]) if final else h3


def _ffn_prompt_kernel(h_ref, osb_ref, oml_ref, p_ref, wout_ref, gffn_ref, wup_ref, cw_ref, cb_ref,
                       wdn_ref, gple_ref, wg_ref, wp_ref, gfin_ref,
                       out_ref, cst_ref, act_ref, carry_ref, *, tiles_per_seq, final):
    tm = h_ref.shape[0]
    h1, hn = _mix_in(h_ref, osb_ref, oml_ref, wout_ref, gffn_ref)

    @pl.when(pl.program_id(0) % tiles_per_seq == 0)
    def _():
        carry_ref[...] = jnp.zeros_like(carry_ref)

    rowi = lax.broadcasted_iota(jnp.int32, (SUBLANES, FF_CHUNK), 0)
    for c in range(N_FF_CHUNKS):
        cols, gate, up = _ffn_chunk(hn, wup_ref, c)
        prev = carry_ref[:, cols]
        carry_ref[:, cols] = gate[tm - SUBLANES:tm, :]
        cst_ref[:, cols] = gate[tm - (CONV_W - 1):tm, :]
        conv = cb_ref[:, cols] + cw_ref[CONV_W - 1:CONV_W, cols] * gate
        for d in range(1, CONV_W):
            rolled = pltpu.roll(gate, d, axis=0)
            top = jnp.where(rowi < d, pltpu.roll(prev, d, axis=0), rolled[0:SUBLANES])
            shifted = jnp.concatenate([top, rolled[SUBLANES:]], axis=0)
            conv = conv + cw_ref[CONV_W - 1 - d:CONV_W - d, cols] * shifted
        act_ref[:, cols] = (_gelu(conv) * up).astype(BF16)
    _ple_out(h1, act_ref, wdn_ref, p_ref, gple_ref, wg_ref, wp_ref, gfin_ref, out_ref, final)


def _ffn_sample_kernel(h_ref, osb_ref, oml_ref, p_ref, pre_ref, wout_ref, gffn_ref, wup_ref, cw_ref, cb_ref,
                       wdn_ref, gple_ref, wg_ref, wp_ref, gfin_ref,
                       out_ref, cst_ref, act_ref, *, n_tok, n_seq, final):
    h1, hn = _mix_in(h_ref, osb_ref, oml_ref, wout_ref, gffn_ref)
    n_pre = CONV_W - 1
    for c in range(N_FF_CHUNKS):
        cols, gate, up = _ffn_chunk(hn, wup_ref, c)
        gp = [pre_ref[j * n_seq:(j + 1) * n_seq, cols] for j in range(n_pre)]
        gp += [gate[t * n_seq:(t + 1) * n_seq] for t in range(n_tok)]
        for j in range(n_pre):
            cst_ref[j * n_seq:(j + 1) * n_seq, cols] = gp[n_tok + j]
        conv = []
        for t in range(n_tok):
            acc = cb_ref[:, cols]
            for j in range(CONV_W):
                acc = acc + cw_ref[j:j + 1, cols] * gp[t + j]
            conv.append(acc)
        act_ref[:, cols] = (_gelu(jnp.concatenate(conv, axis=0)) * up).astype(BF16)
    _ple_out(h1, act_ref, wdn_ref, p_ref, gple_ref, wg_ref, wp_ref, gfin_ref, out_ref, final)


def _params(*sem):
    return pltpu.CompilerParams(dimension_semantics=sem, vmem_limit_bytes=VMEM_LIMIT)


def _proj_prompt(h, g, win, wkvt, ws_pair, bias_full, gnm, seq, prev, layer):
    n = h.shape[0]
    tps = seq // TM
    n_prev = 0 if prev is None else prev[0].shape[0]
    tile = lambda w: pl.BlockSpec((TM, w), lambda i: (i, 0))
    stack = lambda layers: pl.BlockSpec((layers, None, D_SB, TM), lambda i: (0, i // tps, 0, i % tps))
    kvt_shape = jax.ShapeDtypeStruct((n_prev + 1, n // seq, D_SB, seq), F32)
    return pl.pallas_call(
        functools.partial(_proj_prompt_kernel, n_prev=n_prev),
        grid=(n // TM,),
        in_specs=[tile(D_MODEL), _of_layer((1, D_MODEL), layer), _of_layer((D_MODEL, D_IN), layer),
                  _of_layer((2 * D_SB, D_MODEL), layer), _resident(ws_pair.shape), _resident(bias_full.shape),
                  _of_layer((1, D_MLP), layer)] + [stack(n_prev)] * (2 if n_prev else 0),
        out_specs=[tile(D_SB), tile(D_SB), pl.BlockSpec((TM // TQ, D_SB, TQ), lambda i: (i, 0, 0)),
                   tile(D_MLP), stack(n_prev + 1), stack(n_prev + 1)],
        out_shape=[jax.ShapeDtypeStruct((n, D_SB), BF16), jax.ShapeDtypeStruct((n, D_SB), BF16),
                   jax.ShapeDtypeStruct((n // TQ, D_SB, TQ), BF16), jax.ShapeDtypeStruct((n, D_MLP), BF16),
                   kvt_shape, kvt_shape],
        compiler_params=_params("arbitrary"),
        name="proj_prompt",
    )(h, g, win, wkvt, ws_pair, bias_full, gnm, *(prev or ()))


def _proj_sample(h, g, win, wrow, brow, gnm, n_tok, n_seq, layer):
    n = h.shape[0]
    full = lambda w: pl.BlockSpec((n, w), lambda i: (0, 0))
    return pl.pallas_call(
        functools.partial(_proj_sample_kernel, n_tok=n_tok, n_seq=n_seq),
        grid=(1,),
        in_specs=[full(D_MODEL), _of_layer((1, D_MODEL), layer), _of_layer((D_MODEL, D_IN), layer),
                  _resident(wrow.shape), _resident(brow.shape), _of_layer((1, D_MLP), layer)],
        out_specs=[full(D_SB)] * 5,
        out_shape=[jax.ShapeDtypeStruct((n, D_SB), F32)] * 4 + [jax.ShapeDtypeStruct((n, D_MLP), BF16)],
        compiler_params=_params("arbitrary"),
        name="proj_sample",
    )(h, g, win, wrow, brow, gnm)


def _sb_prompt(bias, q, kb, vtb, gn, seq, layer):
    n = q.shape[0]
    nq = seq // TQ
    return pl.pallas_call(
        functools.partial(_sb_prompt_kernel, layer=layer),
        grid=(n // seq, nq),
        in_specs=[pl.BlockSpec(memory_space=pltpu.SMEM),
                  pl.BlockSpec((TQ, D_SB), lambda b, i: (b * nq + i, 0)),
                  pl.BlockSpec((seq, D_SB), lambda b, i: (b, 0)),
                  pl.BlockSpec((nq, D_SB, TQ), lambda b, i: (b, 0, 0)),
                  _of_layer((1, D_SB), layer)],
        out_specs=pl.BlockSpec((TQ, D_SB), lambda b, i: (b * nq + i, 0)),
        out_shape=jax.ShapeDtypeStruct((n, D_SB), BF16),
        scratch_shapes=[pltpu.VMEM((HEAD_PAIRS, LANES, 2 * TQ), BF16), pltpu.VMEM((D_SB, TQ), F32)],
        compiler_params=_params("arbitrary", "arbitrary"),
        name="sb_prompt",
    )(bias, q, kb, vtb, gn)


def _sb_sample(page_table, bias_rows, q, k_new, v_new, gn, cache_kt, cache_vt, layer):
    n_seq, n_pages = page_table.shape
    n_tok = q.shape[1]
    page = lambda j: pl.BlockSpec((None, None, D_SB, PAGE_SIZE), lambda b, pt, j=j: (layer, pt[b, j], 0, 0))
    per_seq = lambda rows: pl.BlockSpec((None, rows, D_SB), lambda b, pt: (b, 0, 0))
    const = lambda shape: pl.BlockSpec(shape, lambda b, pt: (0,) * len(shape), pipeline_mode=pl.Buffered(1))
    grid_spec = pltpu.PrefetchScalarGridSpec(
        num_scalar_prefetch=1,
        grid=(n_seq,),
        in_specs=[const(bias_rows.shape), per_seq(n_tok), per_seq(SUBLANES), per_seq(SUBLANES),
                  _of_layer((1, D_SB), layer)] + [page(j) for j in range(n_pages)] * 2,
        out_specs=per_seq(n_tok),
        scratch_shapes=[pltpu.VMEM((D_SB, n_pages * PAGE_SIZE), BF16)] * 2,
    )
    return pl.pallas_call(
        functools.partial(_sb_sample_kernel, n_tok=n_tok, n_pages=n_pages),
        grid_spec=grid_spec,
        out_shape=jax.ShapeDtypeStruct((n_seq, n_tok, D_SB), F32),
        compiler_params=_params("arbitrary"),
        name="sb_sample",
    )(page_table, bias_rows, q, k_new, v_new, gn, *([cache_kt] * n_pages), *([cache_vt] * n_pages))


def _weight_specs(layer):
    shapes = [(D_MODEL, D_MODEL), (1, D_MODEL), (D_MODEL, 2 * D_FF), (CONV_W, D_FF), (1, D_FF), (D_FF, D_MODEL),
              (1, D_MODEL), (D_MODEL, D_MODEL), (PLE_DIM, D_MODEL)]
    return [_of_layer(s, layer) for s in shapes] + [_resident((1, D_MODEL))]


def _ffn_prompt(h, osb, oml, p, weights, seq, layer, final):
    n = h.shape[0]
    tile = lambda w: pl.BlockSpec((TM, w), lambda i: (i, 0))
    tiles_per_seq = seq // TM
    return pl.pallas_call(
        functools.partial(_ffn_prompt_kernel, tiles_per_seq=tiles_per_seq, final=final),
        grid=(n // TM,),
        in_specs=[tile(D_MODEL), tile(D_SB), tile(D_MLP),
                  pl.BlockSpec((None, TM, PLE_DIM), lambda i: (layer, i, 0))] + _weight_specs(layer),
        out_specs=[tile(D_MODEL),
                   pl.BlockSpec((None, CONV_W - 1, D_FF), lambda i: (i // tiles_per_seq, 0, 0))],
        out_shape=[jax.ShapeDtypeStruct((n, D_MODEL), F32),
                   jax.ShapeDtypeStruct((n // seq, CONV_W - 1, D_FF), F32)],
        scratch_shapes=[pltpu.VMEM((TM, D_FF), BF16), pltpu.VMEM((SUBLANES, D_FF), F32)],
        compiler_params=_params("arbitrary"),
        name="ffn_prompt",
    )(h, osb, oml, p, *weights)


def _ffn_sample(h, osb, oml, p, prefix, weights, n_tok, n_seq, layer, final):
    n = h.shape[0]
    full = lambda rows, w: pl.BlockSpec((rows, w), lambda i: (0, 0))
    n_pre = (CONV_W - 1) * n_seq
    return pl.pallas_call(
        functools.partial(_ffn_sample_kernel, n_tok=n_tok, n_seq=n_seq, final=final),
        grid=(1,),
        in_specs=[full(n, D_MODEL), full(n, D_SB), full(n, D_MLP), full(n, PLE_DIM),
                  full(n_pre, D_FF)] + _weight_specs(layer),
        out_specs=[full(n, D_MODEL), full(n_pre, D_FF)],
        out_shape=[jax.ShapeDtypeStruct((n, D_MODEL), F32), jax.ShapeDtypeStruct((n_pre, D_FF), F32)],
        scratch_shapes=[pltpu.VMEM((n, D_FF), BF16)],
        compiler_params=_params("arbitrary"),
        name="ffn_sample",
    )(h, osb, oml, p, prefix, *weights)


def kernel(x_prompt, x_sample, cache_k, cache_v, state_conv, page_table, p_prompt, p_sample, norm_mix_g, w_in, sb_bias, gn_sb, gn_mlp, w_spatial, b_spatial, w_out, norm_ffn_g, w_up, conv_w, conv_b, w_down, norm_ple_g, w_ple_gate, w_ple_proj, final_norm_g):
    depth = w_in.shape[0]
    bp, seq, _ = x_prompt.shape
    n_seq, n_tok, _ = x_sample.shape
    n_phys = cache_k.shape[1]

    to_tok_major = lambda x: jnp.swapaxes(x, 0, 1).reshape((x.shape[0] * x.shape[1],) + x.shape[2:])
    to_seq_major = lambda x, t: jnp.swapaxes(x.reshape((t, n_seq) + x.shape[1:]), 0, 1)
    row = lambda x: x.reshape(1, -1)

    hp = x_prompt.reshape(bp * seq, D_MODEL)
    hs = to_tok_major(x_sample)
    page_t = lambda c: jnp.transpose(c, (0, 1, 3, 4, 2)).reshape(depth, n_phys, D_SB, PAGE_SIZE)
    ckt, cvt = page_t(cache_k), page_t(cache_v)
    rows = lambda x: x.reshape(depth, 1, -1)
    win = w_in.astype(BF16)
    wkvt = jnp.swapaxes(w_in[:, :, D_SB:3 * D_SB], 1, 2).astype(BF16)
    weights = (w_out.astype(BF16), rows(norm_ffn_g), w_up.astype(BF16), conv_w, rows(conv_b),
               w_down.astype(BF16), rows(norm_ple_g), w_ple_gate.astype(BF16), w_ple_proj.astype(BF16),
               row(final_norm_g))
    gmix, gnm, gns = rows(norm_mix_g), rows(gn_mlp), rows(gn_sb)
    pp = p_prompt.reshape(depth, bp * seq, PLE_DIM)

    ks_l, vs_l, cp_l, cs_l, ms_l = [], [], [], [], []
    kvt = None
    for i in range(depth):
        final = i == depth - 1

        ws = w_spatial[i]
        ws_pair = jnp.concatenate([ws[0::2], ws[1::2]], axis=2)
        bias_full = jnp.repeat(b_spatial[i].T, GROUP_DIM, axis=1)
        q, kb, vtb, oml, kt_all, vt_all = _proj_prompt(hp, gmix, win, wkvt, ws_pair, bias_full, gnm, seq, kvt, i)
        kvt = (kt_all, vt_all)
        osb = _sb_prompt(sb_bias, q, kb, vtb, gns, seq, i)
        hp, conv_p = _ffn_prompt(hp, osb, oml, pp, weights, seq, i, final)
        cp_l.append(conv_p)

        wrow = jnp.repeat(ws[:, :n_tok, :n_tok].transpose(1, 2, 0).reshape(n_tok * n_tok, N_GROUPS),
                          GROUP_DIM, axis=1)
        brow = jnp.repeat(b_spatial[i][:, :n_tok].T, GROUP_DIM, axis=1)
        q, k, v, vm, oml = _proj_sample(hs, gmix, win, wrow, brow, gnm, n_tok, n_seq, i)
        pad_rows = lambda x: jnp.pad(to_seq_major(x, n_tok), ((0, 0), (0, SUBLANES - n_tok), (0, 0)))
        bias_rows = jnp.broadcast_to(jnp.tile(sb_bias[i], n_tok)[:, None], (n_tok * N_HEADS, LANES))
        osb = _sb_sample(page_table, bias_rows, to_seq_major(q, n_tok), pad_rows(k), pad_rows(v), gns,
                         ckt, cvt, i)
        osb = to_tok_major(osb).astype(BF16)
        hs, conv_s = _ffn_sample(hs, osb, oml, to_tok_major(p_sample[i]), to_tok_major(state_conv[i]),
                                 weights, n_tok, n_seq, i, final)
        ks_l.append(to_seq_major(k, n_tok)); vs_l.append(to_seq_major(v, n_tok))
        cs_l.append(to_seq_major(conv_s, CONV_W - 1)); ms_l.append(to_seq_major(vm, n_tok))

    heads = lambda x: x.reshape(x.shape[:-1] + (N_HEADS, HEAD_DIM))
    from_t = lambda x: jnp.transpose(x.reshape(depth, bp, N_HEADS, HEAD_DIM, seq), (0, 1, 4, 2, 3))
    return (hp.reshape(bp, seq, D_MODEL),
            to_seq_major(hs, n_tok),
            from_t(kvt[0]),
            from_t(kvt[1]),
            heads(jnp.stack(ks_l)),
            heads(jnp.stack(vs_l)),
            jnp.stack(cp_l),
            jnp.stack(cs_l),
            jnp.stack(ms_l))
```

```python
import functools
import math

import jax
import jax.numpy as jnp
from jax import lax
from jax.experimental import pallas as pl
from jax.experimental.pallas import tpu as pltpu

F32 = jnp.float32
BF16 = jnp.bfloat16

D_MODEL = 1024
N_HEADS = 8
HEAD_DIM = 64
D_SB = N_HEADS * HEAD_DIM
N_GROUPS = 8
GROUP_DIM = 64
D_MLP = N_GROUPS * GROUP_DIM
D_IN = 3 * D_SB + 2 * D_MLP
CHUNK = 128
D_FF = 2816
CONV_W = 3
PLE_DIM = 256
PAGE_SIZE = 128
EPS = 1e-6

LANES = 128
SUBLANES = 8
HEAD_PAIRS = D_SB // LANES
FF_CHUNK = 256
N_FF_CHUNKS = D_FF // FF_CHUNK
TM = 512
TQ = 256
VMEM_LIMIT = 56 * 1024 * 1024


def _bdot(a, b):
    return jnp.dot(a, b, preferred_element_type=F32)


def _rms(x, g):
    ms = jnp.mean(x * x, axis=-1, keepdims=True)
    return x * lax.rsqrt(ms + EPS) * g


def _gelu(x):
    c = math.sqrt(2.0 / math.pi)
    return x * (0.5 * (1.0 + jnp.tanh(c * (x + 0.044715 * (x * x * x)))))


def _resident(shape):
    nd = len(shape)
    return pl.BlockSpec(shape, lambda *_: (0,) * nd, pipeline_mode=pl.Buffered(1))


def _of_layer(shape, layer):
    nd = len(shape)
    return pl.BlockSpec((None,) + tuple(shape), lambda *_: (layer,) + (0,) * nd, pipeline_mode=pl.Buffered(1))


def _project(h_ref, g_ref, win_ref):
    hn = _rms(h_ref[...], g_ref[...]).astype(BF16)
    seg = lambda j: _bdot(hn, win_ref[:, j * D_SB:(j + 1) * D_SB])
    return seg(0) * (HEAD_DIM ** -0.5), seg(1), seg(2), _gelu(seg(3)), _gelu(seg(4))


def _proj_prompt_kernel(h_ref, g_ref, win_ref, wkvt_ref, ws_ref, bs_ref, gnm_ref, *refs, n_prev):
    prev = refs[:2] if n_prev else None
    q_ref, kb_ref, vtb_ref, om_ref, kt_ref, vt_ref = refs[2 if n_prev else 0:]
    hn = _rms(h_ref[...], g_ref[...]).astype(BF16)
    seg = lambda j: _bdot(hn, win_ref[:, j * D_SB:(j + 1) * D_SB])
    q_ref[...] = (seg(0) * (HEAD_DIM ** -0.5)).astype(BF16)
    kb_ref[...] = seg(1).astype(BF16)
    u, vm = _gelu(seg(3)), _gelu(seg(4))
    kt = _dot_nt(wkvt_ref[0:D_SB, :], hn)
    vt = _dot_nt(wkvt_ref[D_SB:2 * D_SB, :], hn)
    if n_prev:
        kt_ref[0:n_prev] = prev[0][...]
        vt_ref[0:n_prev] = prev[1][...]
    kt_ref[n_prev] = kt
    vt_ref[n_prev] = vt
    for c in range(vtb_ref.shape[0]):
        vtb_ref[c] = vt[:, c * TQ:(c + 1) * TQ].astype(BF16)

    row = lax.broadcasted_iota(jnp.int32, (CHUNK, 2 * CHUNK), 0)
    col = lax.broadcasted_iota(jnp.int32, (CHUNK, 2 * CHUNK), 1)
    tril = (col & (CHUNK - 1)) <= row
    wpair = [jnp.where(tril, ws_ref[p], 0.0).astype(BF16) for p in range(HEAD_PAIRS)]
    lo_half = lax.broadcasted_iota(jnp.int32, (CHUNK, LANES), 1) < GROUP_DIM
    for c in range(h_ref.shape[0] // CHUNK):
        rows = slice(c * CHUNK, (c + 1) * CHUNK)
        parts = []
        for p in range(HEAD_PAIRS):
            s = vm[rows, p * LANES:(p + 1) * LANES]
            rhs = jnp.concatenate([jnp.where(lo_half, s, 0.0).astype(BF16),
                                   jnp.where(lo_half, 0.0, s).astype(BF16)], axis=0)
            parts.append(_bdot(wpair[p], rhs))
        mixed = jnp.concatenate(parts, axis=1) + bs_ref[...]
        om_ref[rows, :] = _rms(u[rows] * mixed, gnm_ref[...]).astype(BF16)


def _proj_sample_kernel(h_ref, g_ref, win_ref, wrow_ref, brow_ref, gnm_ref,
                        q_ref, k_ref, v_ref, vm_ref, om_ref, *, n_tok, n_seq):
    q, k, v, u, vm = _project(h_ref, g_ref, win_ref)
    q_ref[...] = q
    k_ref[...] = k
    v_ref[...] = v
    vm_ref[...] = vm
    for t in range(n_tok):
        mixed = brow_ref[t:t + 1, :]
        for s in range(t + 1):
            w = wrow_ref[t * n_tok + s:t * n_tok + s + 1, :]
            mixed = mixed + w * vm[s * n_seq:(s + 1) * n_seq]
        rows = slice(t * n_seq, (t + 1) * n_seq)
        om_ref[rows, :] = _rms(u[rows] * mixed, gnm_ref[...]).astype(BF16)


def _dot_nt(a, b):
    return lax.dot_general(a, b, (((1,), (1,)), ((), ())), preferred_element_type=F32)


def _sb_suffix(z, ucat, mask, key_axis=1):
    lg = jnp.log(1.0 + jnp.exp(-jnp.abs(z)))
    log_b = jnp.minimum(z, 0.0) - lg
    log_1mb = log_b - z
    if mask is not None:
        log_1mb = jnp.where(mask, log_1mb, 0.0)
    hi = log_1mb.astype(BF16)
    lo = (log_1mb - hi.astype(F32)).astype(BF16)
    split = jnp.concatenate([hi, lo], axis=key_axis)
    suffix = _bdot(split, ucat) if key_axis == 1 else _bdot(ucat, split)
    first = (slice(None), slice(0, 1)) if key_axis == 1 else (slice(0, 1), slice(None))
    return log_b, suffix, suffix[first] + log_1mb[first]


def _sb_weights(z, carry, ucat, mask):
    log_b, suffix, total = _sb_suffix(z, ucat, mask)
    a = jnp.exp(log_b + suffix + carry)
    if mask is not None:
        a = jnp.where(mask, a, 0.0)
    return a, carry + total


def _suffix_ones(tk, key_axis=1):
    r = lax.broadcasted_iota(jnp.int32, (tk, tk), 0)
    c = lax.broadcasted_iota(jnp.int32, (tk, tk), 1)
    u = jnp.where(r > c if key_axis == 1 else c > r, 1.0, 0.0).astype(BF16)
    return jnp.concatenate([u, u], axis=1 - key_axis)


def _sb_prompt_kernel(bias_ref, q_ref, k_ref, vt_ref, gn_ref, o_ref, qmt_ref, acc_ref, *, layer):
    tq = q_ref.shape[0]
    qi = pl.program_id(1)
    ucat = _suffix_ones(tq, key_axis=0)
    r = lax.broadcasted_iota(jnp.int32, (tq, 2 * tq), 0)
    c = lax.broadcasted_iota(jnp.int32, (tq, 2 * tq), 1)
    causal = r < (c & (tq - 1))
    lo_half = lax.broadcasted_iota(jnp.int32, (LANES, tq), 0) < HEAD_DIM
    pair = [slice(p * LANES, (p + 1) * LANES) for p in range(HEAD_PAIRS)]

    for p in range(HEAD_PAIRS):
        q2t = q_ref[:, pair[p]].astype(F32).T
        qmt_ref[p] = jnp.concatenate([jnp.where(lo_half, q2t, 0.0), jnp.where(lo_half, 0.0, q2t)],
                                     axis=1).astype(BF16)

    def key_blocks(js, carries, mask, first):
        work = [(j, p) for j in js for p in range(HEAD_PAIRS)]
        logits, terms = [], []
        for j, p in work:
            start = pl.multiple_of(j * tq, tq)
            s = _bdot(k_ref[pl.ds(start, tq), pair[p]], qmt_ref[p])
            logits.append(jnp.concatenate([s[:, :tq] + bias_ref[layer, 2 * p],
                                           s[:, tq:] + bias_ref[layer, 2 * p + 1]], axis=1))
        for z in logits:
            terms.append(_sb_suffix(z, ucat, mask, key_axis=0))
        carries = list(carries)
        for n, (j, p) in enumerate(work):
            log_b, suffix, total = terms[n]
            a = jnp.exp(log_b + suffix)
            if mask is not None:
                a = jnp.where(mask, a, 0.0)
            o = _bdot(vt_ref[j, pair[p], :], a.astype(BF16)) * jnp.exp(carries[p])
            o = jnp.where(lo_half, o[:, :tq], o[:, tq:])
            acc_ref[pair[p], :] = o if first and n < HEAD_PAIRS else acc_ref[pair[p], :] + o
            carries[p] = carries[p] + total
        return tuple(carries)

    carries = key_blocks([qi], (jnp.zeros((1, 2 * tq), F32),) * HEAD_PAIRS, causal, True)
    carries = lax.fori_loop(0, lax.shift_right_logical(qi, 1),
                            lambda i, c: key_blocks([qi - 1 - 2 * i, qi - 2 - 2 * i], c, None, False), carries)
    lax.fori_loop(0, qi & 1, lambda i, c: key_blocks([0], c, None, False), carries)
    o_ref[...] = _rms(acc_ref[...].T, gn_ref[...]).astype(BF16)


def _sb_sample_kernel(pt_ref, bias_ref, q_ref, kn_ref, vn_ref, gn_ref, ck_hbm, cv_hbm, o_ref,
                      kbuf, vbuf, sem, kt_ref, vt_ref, *, n_tok, n_pages, layer):
    b = pl.program_id(0)
    slot = b & 1

    def page_copies(seq, slot):
        copies = []
        for j in range(n_pages):
            page = pt_ref[seq, j]
            copies.append(pltpu.make_async_copy(ck_hbm.at[layer, page], kbuf.at[slot, j], sem.at[slot, 0]))
            copies.append(pltpu.make_async_copy(cv_hbm.at[layer, page], vbuf.at[slot, j], sem.at[slot, 1]))
        return copies

    @pl.when(b == 0)
    def _():
        for c in page_copies(0, 0):
            c.start()

    @pl.when(b + 1 < pl.num_programs(0))
    def _():
        for c in page_copies(b + 1, 1 - slot):
            c.start()

    for c in page_copies(b, slot):
        c.wait()

    m = n_tok * N_HEADS
    tk = 2 * PAGE_SIZE
    past = n_pages * PAGE_SIZE
    head_of_lane = lax.broadcasted_iota(jnp.int32, (N_HEADS, D_SB), 1) // HEAD_DIM
    own = head_of_lane == lax.broadcasted_iota(jnp.int32, (N_HEADS, D_SB), 0)
    q = q_ref[...]
    qbd = jnp.concatenate(
        [jnp.where(own, jnp.broadcast_to(q[t:t + 1, :], (N_HEADS, D_SB)), 0.0) for t in range(n_tok)],
        axis=0).astype(BF16)
    bias = bias_ref[:, 0:1]

    for j in range(n_pages):
        kt_ref[:, j * PAGE_SIZE:(j + 1) * PAGE_SIZE] = kbuf[slot, j].astype(BF16)
        vt_ref[:, j * PAGE_SIZE:(j + 1) * PAGE_SIZE] = vbuf[slot, j].astype(BF16)

    pad = jnp.zeros((PAGE_SIZE - SUBLANES, D_SB), F32)
    kn = jnp.concatenate([kn_ref[...], pad], axis=0).astype(BF16)
    vn = jnp.concatenate([vn_ref[...], pad], axis=0).astype(BF16)
    tok = lax.broadcasted_iota(jnp.int32, (m, PAGE_SIZE), 0) // N_HEADS
    mask = lax.broadcasted_iota(jnp.int32, (m, PAGE_SIZE), 1) < tok
    a, carry = _sb_weights(_dot_nt(qbd, kn) + bias, jnp.zeros((m, 1), F32), _suffix_ones(PAGE_SIZE), mask)
    acc = _bdot(a.astype(BF16), vn)

    z = _bdot(qbd, kt_ref[...]) + bias
    ucat = _suffix_ones(tk)
    blocks = [None] * (past // tk)
    for j in reversed(range(past // tk)):
        a, carry = _sb_weights(z[:, j * tk:(j + 1) * tk], carry, ucat, None)
        blocks[j] = a.astype(BF16)
    acc = acc + _dot_nt(jnp.concatenate(blocks, axis=1), vt_ref[...])

    for t in range(n_tok):
        blk = jnp.where(own, acc[t * N_HEADS:(t + 1) * N_HEADS, :], 0.0)
        row = jnp.sum(blk, axis=0, keepdims=True)
        o_ref[t:t + 1, :] = _rms(row, gn_ref[...])


def _mix_in(h_ref, osb_ref, oml_ref, wout_ref, gffn_ref):
    o = jnp.concatenate([osb_ref[...], oml_ref[...]], axis=1)
    h1 = h_ref[...] + _bdot(o, wout_ref[...])
    return h1, _rms(h1, gffn_ref[...]).astype(BF16)


def _ffn_chunk(hn, wup_ref, c):
    cols = slice(c * FF_CHUNK, (c + 1) * FF_CHUNK)
    gate = _bdot(hn, wup_ref[:, cols])
    up = _bdot(hn, wup_ref[:, D_FF + c * FF_CHUNK:D_FF + (c + 1) * FF_CHUNK])
    return cols, gate, up


def _ple_out(h1, act_ref, wdn_ref, p_ref, gple_ref, wg_ref, wp_ref, gfin_ref, out_ref, final):
    h2 = h1 + _bdot(act_ref[...], wdn_ref[...])
    gate = jax.nn.sigmoid(_bdot(_rms(h2, gple_ref[...]).astype(BF16), wg_ref[...]))
    h3 = h2 + gate * _bdot(p_ref[...].astype(BF16), wp_ref[...])
    out_ref[...] = _rms(h3, gfin_ref[...]) if final else h3


def _ffn_prompt_kernel(h_ref, osb_ref, oml_ref, p_ref, wout_ref, gffn_ref, wup_ref, cw_ref, cb_ref,
                       wdn_ref, gple_ref, wg_ref, wp_ref, gfin_ref,
                       out_ref, cst_ref, act_ref, carry_ref, *, tiles_per_seq, final):
    tm = h_ref.shape[0]
    h1, hn = _mix_in(h_ref, osb_ref, oml_ref, wout_ref, gffn_ref)

    @pl.when(pl.program_id(0) % tiles_per_seq == 0)
    def _():
        carry_ref[...] = jnp.zeros_like(carry_ref)

    rowi = lax.broadcasted_iota(jnp.int32, (SUBLANES, FF_CHUNK), 0)
    for c in range(N_FF_CHUNKS):
        cols, gate, up = _ffn_chunk(hn, wup_ref, c)
        prev = carry_ref[:, cols]
        carry_ref[:, cols] = gate[tm - SUBLANES:tm, :]
        cst_ref[:, cols] = gate[tm - (CONV_W - 1):tm, :]
        conv = cb_ref[:, cols] + cw_ref[CONV_W - 1:CONV_W, cols] * gate
        for d in range(1, CONV_W):
            rolled = pltpu.roll(gate, d, axis=0)
            top = jnp.where(rowi < d, pltpu.roll(prev, d, axis=0), rolled[0:SUBLANES])
            shifted = jnp.concatenate([top, rolled[SUBLANES:]], axis=0)
            conv = conv + cw_ref[CONV_W - 1 - d:CONV_W - d, cols] * shifted
        act_ref[:, cols] = (_gelu(conv) * up).astype(BF16)
    _ple_out(h1, act_ref, wdn_ref, p_ref, gple_ref, wg_ref, wp_ref, gfin_ref, out_ref, final)


def _ffn_sample_kernel(h_ref, osb_ref, oml_ref, p_ref, pre_ref, wout_ref, gffn_ref, wup_ref, cw_ref, cb_ref,
                       wdn_ref, gple_ref, wg_ref, wp_ref, gfin_ref,
                       out_ref, cst_ref, act_ref, *, n_tok, n_seq, final):
    h1, hn = _mix_in(h_ref, osb_ref, oml_ref, wout_ref, gffn_ref)
    n_pre = CONV_W - 1
    for c in range(N_FF_CHUNKS):
        cols, gate, up = _ffn_chunk(hn, wup_ref, c)
        gp = [pre_ref[j * n_seq:(j + 1) * n_seq, cols] for j in range(n_pre)]
        gp += [gate[t * n_seq:(t + 1) * n_seq] for t in range(n_tok)]
        for j in range(n_pre):
            cst_ref[j * n_seq:(j + 1) * n_seq, cols] = gp[n_tok + j]
        conv = []
        for t in range(n_tok):
            acc = cb_ref[:, cols]
            for j in range(CONV_W):
                acc = acc + cw_ref[j:j + 1, cols] * gp[t + j]
            conv.append(acc)
        act_ref[:, cols] = (_gelu(jnp.concatenate(conv, axis=0)) * up).astype(BF16)
    _ple_out(h1, act_ref, wdn_ref, p_ref, gple_ref, wg_ref, wp_ref, gfin_ref, out_ref, final)


def _params(*sem):
    return pltpu.CompilerParams(dimension_semantics=sem, vmem_limit_bytes=VMEM_LIMIT)


def _proj_prompt(h, g, win, wkvt, ws_pair, bias_full, gnm, seq, prev, layer):
    n = h.shape[0]
    tps = seq // TM
    n_prev = 0 if prev is None else prev[0].shape[0]
    tile = lambda w: pl.BlockSpec((TM, w), lambda i: (i, 0))
    stack = lambda layers: pl.BlockSpec((layers, None, D_SB, TM), lambda i: (0, i // tps, 0, i % tps))
    kvt_shape = jax.ShapeDtypeStruct((n_prev + 1, n // seq, D_SB, seq), F32)
    return pl.pallas_call(
        functools.partial(_proj_prompt_kernel, n_prev=n_prev),
        grid=(n // TM,),
        in_specs=[tile(D_MODEL), _of_layer((1, D_MODEL), layer), _of_layer((D_MODEL, D_IN), layer),
                  _of_layer((2 * D_SB, D_MODEL), layer), _resident(ws_pair.shape), _resident(bias_full.shape),
                  _of_layer((1, D_MLP), layer)] + [stack(n_prev)] * (2 if n_prev else 0),
        out_specs=[tile(D_SB), tile(D_SB), pl.BlockSpec((TM // TQ, D_SB, TQ), lambda i: (i, 0, 0)),
                   tile(D_MLP), stack(n_prev + 1), stack(n_prev + 1)],
        out_shape=[jax.ShapeDtypeStruct((n, D_SB), BF16), jax.ShapeDtypeStruct((n, D_SB), BF16),
                   jax.ShapeDtypeStruct((n // TQ, D_SB, TQ), BF16), jax.ShapeDtypeStruct((n, D_MLP), BF16),
                   kvt_shape, kvt_shape],
        compiler_params=_params("arbitrary"),
        name="proj_prompt",
    )(h, g, win, wkvt, ws_pair, bias_full, gnm, *(prev or ()))


def _proj_sample(h, g, win, wrow, brow, gnm, n_tok, n_seq, layer):
    n = h.shape[0]
    full = lambda w: pl.BlockSpec((n, w), lambda i: (0, 0))
    return pl.pallas_call(
        functools.partial(_proj_sample_kernel, n_tok=n_tok, n_seq=n_seq),
        grid=(1,),
        in_specs=[full(D_MODEL), _of_layer((1, D_MODEL), layer), _of_layer((D_MODEL, D_IN), layer),
                  _resident(wrow.shape), _resident(brow.shape), _of_layer((1, D_MLP), layer)],
        out_specs=[full(D_SB)] * 5,
        out_shape=[jax.ShapeDtypeStruct((n, D_SB), F32)] * 4 + [jax.ShapeDtypeStruct((n, D_MLP), BF16)],
        compiler_params=_params("arbitrary"),
        name="proj_sample",
    )(h, g, win, wrow, brow, gnm)


def _sb_prompt(bias, q, kb, vtb, gn, seq, layer):
    n = q.shape[0]
    nq = seq // TQ
    return pl.pallas_call(
        functools.partial(_sb_prompt_kernel, layer=layer),
        grid=(n // seq, nq),
        in_specs=[pl.BlockSpec(memory_space=pltpu.SMEM),
                  pl.BlockSpec((TQ, D_SB), lambda b, i: (b * nq + i, 0)),
                  pl.BlockSpec((seq, D_SB), lambda b, i: (b, 0)),
                  pl.BlockSpec((nq, D_SB, TQ), lambda b, i: (b, 0, 0)),
                  _of_layer((1, D_SB), layer)],
        out_specs=pl.BlockSpec((TQ, D_SB), lambda b, i: (b * nq + i, 0)),
        out_shape=jax.ShapeDtypeStruct((n, D_SB), BF16),
        scratch_shapes=[pltpu.VMEM((HEAD_PAIRS, LANES, 2 * TQ), BF16), pltpu.VMEM((D_SB, TQ), F32)],
        compiler_params=_params("arbitrary", "arbitrary"),
        name="sb_prompt",
    )(bias, q, kb, vtb, gn)


def _sb_sample(page_table, bias_rows, q, k_new, v_new, gn, cache_kt, cache_vt, layer):
    n_seq, n_pages = page_table.shape
    n_tok = q.shape[1]
    per_seq = lambda rows: pl.BlockSpec((None, rows, D_SB), lambda b, pt: (b, 0, 0))
    const = lambda shape: pl.BlockSpec(shape, lambda b, pt: (0,) * len(shape), pipeline_mode=pl.Buffered(1))
    in_hbm = pl.BlockSpec(memory_space=pl.ANY)
    page_slots = pltpu.VMEM((2, n_pages, D_SB, PAGE_SIZE), F32)
    grid_spec = pltpu.PrefetchScalarGridSpec(
        num_scalar_prefetch=1,
        grid=(n_seq,),
        in_specs=[const(bias_rows.shape), per_seq(n_tok), per_seq(SUBLANES), per_seq(SUBLANES),
                  _of_layer((1, D_SB), layer), in_hbm, in_hbm],
        out_specs=per_seq(n_tok),
        scratch_shapes=[page_slots, page_slots, pltpu.SemaphoreType.DMA((2, 2)),
                        pltpu.VMEM((D_SB, n_pages * PAGE_SIZE), BF16),
                        pltpu.VMEM((D_SB, n_pages * PAGE_SIZE), BF16)],
    )
    return pl.pallas_call(
        functools.partial(_sb_sample_kernel, n_tok=n_tok, n_pages=n_pages, layer=layer),
        grid_spec=grid_spec,
        out_shape=jax.ShapeDtypeStruct((n_seq, n_tok, D_SB), F32),
        compiler_params=_params("arbitrary"),
        name="sb_sample",
    )(page_table, bias_rows, q, k_new, v_new, gn, cache_kt, cache_vt)


def _weight_specs(layer):
    shapes = [(D_MODEL, D_MODEL), (1, D_MODEL), (D_MODEL, 2 * D_FF), (CONV_W, D_FF), (1, D_FF), (D_FF, D_MODEL),
              (1, D_MODEL), (D_MODEL, D_MODEL), (PLE_DIM, D_MODEL)]
    return [_of_layer(s, layer) for s in shapes] + [_resident((1, D_MODEL))]


def _ffn_prompt(h, osb, oml, p, weights, seq, layer, final):
    n = h.shape[0]
    tile = lambda w: pl.BlockSpec((TM, w), lambda i: (i, 0))
    tiles_per_seq = seq // TM
    return pl.pallas_call(
        functools.partial(_ffn_prompt_kernel, tiles_per_seq=tiles_per_seq, final=final),
        grid=(n // TM,),
        in_specs=[tile(D_MODEL), tile(D_SB), tile(D_MLP),
                  pl.BlockSpec((None, TM, PLE_DIM), lambda i: (layer, i, 0))] + _weight_specs(layer),
        out_specs=[tile(D_MODEL),
                   pl.BlockSpec((None, CONV_W - 1, D_FF), lambda i: (i // tiles_per_seq, 0, 0))],
        out_shape=[jax.ShapeDtypeStruct((n, D_MODEL), F32),
                   jax.ShapeDtypeStruct((n // seq, CONV_W - 1, D_FF), F32)],
        scratch_shapes=[pltpu.VMEM((TM, D_FF), BF16), pltpu.VMEM((SUBLANES, D_FF), F32)],
        compiler_params=_params("arbitrary"),
        name="ffn_prompt",
    )(h, osb, oml, p, *weights)


def _ffn_sample(h, osb, oml, p, prefix, weights, n_tok, n_seq, layer, final):
    n = h.shape[0]
    full = lambda rows, w: pl.BlockSpec((rows, w), lambda i: (0, 0))
    n_pre = (CONV_W - 1) * n_seq
    return pl.pallas_call(
        functools.partial(_ffn_sample_kernel, n_tok=n_tok, n_seq=n_seq, final=final),
        grid=(1,),
        in_specs=[full(n, D_MODEL), full(n, D_SB), full(n, D_MLP), full(n, PLE_DIM),
                  full(n_pre, D_FF)] + _weight_specs(layer),
        out_specs=[full(n, D_MODEL), full(n_pre, D_FF)],
        out_shape=[jax.ShapeDtypeStruct((n, D_MODEL), F32), jax.ShapeDtypeStruct((n_pre, D_FF), F32)],
        scratch_shapes=[pltpu.VMEM((n, D_FF), BF16)],
        compiler_params=_params("arbitrary"),
        name="ffn_sample",
    )(h, osb, oml, p, prefix, *weights)


def kernel(x_prompt, x_sample, cache_k, cache_v, state_conv, page_table, p_prompt, p_sample, norm_mix_g, w_in, sb_bias, gn_sb, gn_mlp, w_spatial, b_spatial, w_out, norm_ffn_g, w_up, conv_w, conv_b, w_down, norm_ple_g, w_ple_gate, w_ple_proj, final_norm_g):
    depth = w_in.shape[0]
    bp, seq, _ = x_prompt.shape
    n_seq, n_tok, _ = x_sample.shape
    n_phys = cache_k.shape[1]

    to_tok_major = lambda x: jnp.swapaxes(x, 0, 1).reshape((x.shape[0] * x.shape[1],) + x.shape[2:])
    to_seq_major = lambda x, t: jnp.swapaxes(x.reshape((t, n_seq) + x.shape[1:]), 0, 1)
    row = lambda x: x.reshape(1, -1)

    hp = x_prompt.reshape(bp * seq, D_MODEL)
    hs = to_tok_major(x_sample)
    page_t = lambda c: jnp.transpose(c, (0, 1, 3, 4, 2)).reshape(depth, n_phys, D_SB, PAGE_SIZE)
    ckt, cvt = page_t(cache_k), page_t(cache_v)
    rows = lambda x: x.reshape(depth, 1, -1)
    win = w_in.astype(BF16)
    wkvt = jnp.swapaxes(w_in[:, :, D_SB:3 * D_SB], 1, 2).astype(BF16)
    weights = (w_out.astype(BF16), rows(norm_ffn_g), w_up.astype(BF16), conv_w, rows(conv_b),
               w_down.astype(BF16), rows(norm_ple_g), w_ple_gate.astype(BF16), w_ple_proj.astype(BF16),
               row(final_norm_g))
    gmix, gnm, gns = rows(norm_mix_g), rows(gn_mlp), rows(gn_sb)
    pp = p_prompt.reshape(depth, bp * seq, PLE_DIM)

    ks_l, vs_l, cp_l, cs_l, ms_l = [], [], [], [], []
    kvt = None
    for i in range(depth):
        final = i == depth - 1

        ws = w_spatial[i]
        ws_pair = jnp.concatenate([ws[0::2], ws[1::2]], axis=2)
        bias_full = jnp.repeat(b_spatial[i].T, GROUP_DIM, axis=1)
        q, kb, vtb, oml, kt_all, vt_all = _proj_prompt(hp, gmix, win, wkvt, ws_pair, bias_full, gnm, seq, kvt, i)
        kvt = (kt_all, vt_all)
        osb = _sb_prompt(sb_bias, q, kb, vtb, gns, seq, i)
        hp, conv_p = _ffn_prompt(hp, osb, oml, pp, weights, seq, i, final)
        cp_l.append(conv_p)

        wrow = jnp.repeat(ws[:, :n_tok, :n_tok].transpose(1, 2, 0).reshape(n_tok * n_tok, N_GROUPS),
                          GROUP_DIM, axis=1)
        brow = jnp.repeat(b_spatial[i][:, :n_tok].T, GROUP_DIM, axis=1)
        q, k, v, vm, oml = _proj_sample(hs, gmix, win, wrow, brow, gnm, n_tok, n_seq, i)
        pad_rows = lambda x: jnp.pad(to_seq_major(x, n_tok), ((0, 0), (0, SUBLANES - n_tok), (0, 0)))
        bias_rows = jnp.broadcast_to(jnp.tile(sb_bias[i], n_tok)[:, None], (n_tok * N_HEADS, LANES))
        osb = _sb_sample(page_table, bias_rows, to_seq_major(q, n_tok), pad_rows(k), pad_rows(v), gns,
                         ckt, cvt, i)
        osb = to_tok_major(osb).astype(BF16)
        hs, conv_s = _ffn_sample(hs, osb, oml, to_tok_major(p_sample[i]), to_tok_major(state_conv[i]),
                                 weights, n_tok, n_seq, i, final)
        ks_l.append(to_seq_major(k, n_tok)); vs_l.append(to_seq_major(v, n_tok))
        cs_l.append(to_seq_major(conv_s, CONV_W - 1)); ms_l.append(to_seq_major(vm, n_tok))

    heads = lambda x: x.reshape(x.shape[:-1] + (N_HEADS, HEAD_DIM))
    from_t = lambda x: jnp.transpose(x.reshape(depth, bp, N_HEADS, HEAD_DIM, seq), (0, 1, 4, 2, 3))
    return (hp.reshape(bp, seq, D_MODEL),
            to_seq_major(hs, n_tok),
            from_t(kvt[0]),
            from_t(kvt[1]),
            heads(jnp.stack(ks_l)),
            heads(jnp.stack(vs_l)),
            jnp.stack(cp_l),
            jnp.stack(cs_l),
            jnp.stack(ms_l))
```

```python
import functools
import math

import jax
import jax.numpy as jnp
from jax import lax
from jax.experimental import pallas as pl
from jax.experimental.pallas import tpu as pltpu

F32 = jnp.float32
BF16 = jnp.bfloat16

D_MODEL = 1024
N_HEADS = 8
HEAD_DIM = 64
D_SB = N_HEADS * HEAD_DIM
N_GROUPS = 8
GROUP_DIM = 64
D_MLP = N_GROUPS * GROUP_DIM
D_IN = 3 * D_SB + 2 * D_MLP
CHUNK = 128
D_FF = 2816
CONV_W = 3
PLE_DIM = 256
PAGE_SIZE = 128
EPS = 1e-6

LANES = 128
SUBLANES = 8
HEAD_PAIRS = D_SB // LANES
FF_CHUNK = 256
N_FF_CHUNKS = D_FF // FF_CHUNK
TM = 512
TQ = 256
VMEM_LIMIT = 56 * 1024 * 1024


def _bdot(a, b):
    return jnp.dot(a, b, preferred_element_type=F32)


def _rms(x, g):
    ms = jnp.mean(x * x, axis=-1, keepdims=True)
    return x * lax.rsqrt(ms + EPS) * g


def _gelu(x):
    c = math.sqrt(2.0 / math.pi)
    return x * (0.5 * (1.0 + jnp.tanh(c * (x + 0.044715 * (x * x * x)))))


def _resident(shape):
    nd = len(shape)
    return pl.BlockSpec(shape, lambda *_: (0,) * nd, pipeline_mode=pl.Buffered(1))


def _of_layer(shape, layer):
    nd = len(shape)
    return pl.BlockSpec((None,) + tuple(shape), lambda *_: (layer,) + (0,) * nd, pipeline_mode=pl.Buffered(1))


def _project(h_ref, g_ref, win_ref):
    hn = _rms(h_ref[...], g_ref[...]).astype(BF16)
    seg = lambda j: _bdot(hn, win_ref[:, j * D_SB:(j + 1) * D_SB])
    return seg(0) * (HEAD_DIM ** -0.5), seg(1), seg(2), _gelu(seg(3)), _gelu(seg(4))


def _proj_prompt_kernel(h_ref, g_ref, win_ref, wkvt_ref, ws_ref, bs_ref, gnm_ref, *refs, n_prev):
    prev = refs[:2] if n_prev else None
    q_ref, kb_ref, vtb_ref, om_ref, kt_ref, vt_ref = refs[2 if n_prev else 0:]
    hn = _rms(h_ref[...], g_ref[...]).astype(BF16)
    seg = lambda j: _bdot(hn, win_ref[:, j * D_SB:(j + 1) * D_SB])
    q_ref[...] = (seg(0) * (HEAD_DIM ** -0.5)).astype(BF16)
    kb_ref[...] = seg(1).astype(BF16)
    u, vm = _gelu(seg(3)), _gelu(seg(4))
    kt = _dot_nt(wkvt_ref[0:D_SB, :], hn)
    vt = _dot_nt(wkvt_ref[D_SB:2 * D_SB, :], hn)
    if n_prev:
        kt_ref[0:n_prev] = prev[0][...]
        vt_ref[0:n_prev] = prev[1][...]
    kt_ref[n_prev] = kt
    vt_ref[n_prev] = vt
    for c in range(vtb_ref.shape[0]):
        vtb_ref[c] = vt[:, c * TQ:(c + 1) * TQ].astype(BF16)

    row = lax.broadcasted_iota(jnp.int32, (CHUNK, 2 * CHUNK), 0)
    col = lax.broadcasted_iota(jnp.int32, (CHUNK, 2 * CHUNK), 1)
    tril = (col & (CHUNK - 1)) <= row
    wpair = [jnp.where(tril, ws_ref[p], 0.0).astype(BF16) for p in range(HEAD_PAIRS)]
    lo_half = lax.broadcasted_iota(jnp.int32, (CHUNK, LANES), 1) < GROUP_DIM
    for c in range(h_ref.shape[0] // CHUNK):
        rows = slice(c * CHUNK, (c + 1) * CHUNK)
        parts = []
        for p in range(HEAD_PAIRS):
            s = vm[rows, p * LANES:(p + 1) * LANES]
            rhs = jnp.concatenate([jnp.where(lo_half, s, 0.0).astype(BF16),
                                   jnp.where(lo_half, 0.0, s).astype(BF16)], axis=0)
            parts.append(_bdot(wpair[p], rhs))
        mixed = jnp.concatenate(parts, axis=1) + bs_ref[...]
        om_ref[rows, :] = _rms(u[rows] * mixed, gnm_ref[...]).astype(BF16)


def _proj_sample_kernel(h_ref, g_ref, win_ref, wrow_ref, brow_ref, gnm_ref,
                        q_ref, k_ref, v_ref, vm_ref, om_ref, *, n_tok, n_seq):
    q, k, v, u, vm = _project(h_ref, g_ref, win_ref)
    q_ref[...] = q
    k_ref[...] = k
    v_ref[...] = v
    vm_ref[...] = vm
    for t in range(n_tok):
        mixed = brow_ref[t:t + 1, :]
        for s in range(t + 1):
            w = wrow_ref[t * n_tok + s:t * n_tok + s + 1, :]
            mixed = mixed + w * vm[s * n_seq:(s + 1) * n_seq]
        rows = slice(t * n_seq, (t + 1) * n_seq)
        om_ref[rows, :] = _rms(u[rows] * mixed, gnm_ref[...]).astype(BF16)


def _dot_nt(a, b):
    return lax.dot_general(a, b, (((1,), (1,)), ((), ())), preferred_element_type=F32)


def _sb_suffix(z, ucat, mask, key_axis=1):
    lg = jnp.log(1.0 + jnp.exp(-jnp.abs(z)))
    log_b = jnp.minimum(z, 0.0) - lg
    log_1mb = log_b - z
    if mask is not None:
        log_1mb = jnp.where(mask, log_1mb, 0.0)
    hi = log_1mb.astype(BF16)
    lo = (log_1mb - hi.astype(F32)).astype(BF16)
    split = jnp.concatenate([hi, lo], axis=key_axis)
    suffix = _bdot(split, ucat) if key_axis == 1 else _bdot(ucat, split)
    first = (slice(None), slice(0, 1)) if key_axis == 1 else (slice(0, 1), slice(None))
    return log_b, suffix, suffix[first] + log_1mb[first]


def _sb_weights(z, carry, ucat, mask):
    log_b, suffix, total = _sb_suffix(z, ucat, mask)
    a = jnp.exp(log_b + suffix + carry)
    if mask is not None:
        a = jnp.where(mask, a, 0.0)
    return a, carry + total


def _suffix_ones(tk, key_axis=1):
    r = lax.broadcasted_iota(jnp.int32, (tk, tk), 0)
    c = lax.broadcasted_iota(jnp.int32, (tk, tk), 1)
    u = jnp.where(r > c if key_axis == 1 else c > r, 1.0, 0.0).astype(BF16)
    return jnp.concatenate([u, u], axis=1 - key_axis)


def _sb_prompt_kernel(bias_ref, q_ref, k_ref, vt_ref, gn_ref, o_ref, qmt_ref, acc_ref, *, layer):
    tq = q_ref.shape[0]
    qi = pl.program_id(1)
    ucat = _suffix_ones(tq, key_axis=0)
    r = lax.broadcasted_iota(jnp.int32, (tq, 2 * tq), 0)
    c = lax.broadcasted_iota(jnp.int32, (tq, 2 * tq), 1)
    causal = r < (c & (tq - 1))
    lo_half = lax.broadcasted_iota(jnp.int32, (LANES, tq), 0) < HEAD_DIM
    pair = [slice(p * LANES, (p + 1) * LANES) for p in range(HEAD_PAIRS)]

    for p in range(HEAD_PAIRS):
        q2t = q_ref[:, pair[p]].astype(F32).T
        qmt_ref[p] = jnp.concatenate([jnp.where(lo_half, q2t, 0.0), jnp.where(lo_half, 0.0, q2t)],
                                     axis=1).astype(BF16)

    def key_blocks(js, carries, mask, first):
        work = [(j, p) for j in js for p in range(HEAD_PAIRS)]
        logits, terms = [], []
        for j, p in work:
            start = pl.multiple_of(j * tq, tq)
            s = _bdot(k_ref[pl.ds(start, tq), pair[p]], qmt_ref[p])
            logits.append(jnp.concatenate([s[:, :tq] + bias_ref[layer, 2 * p],
                                           s[:, tq:] + bias_ref[layer, 2 * p + 1]], axis=1))
        for z in logits:
            terms.append(_sb_suffix(z, ucat, mask, key_axis=0))
        carries = list(carries)
        for n, (j, p) in enumerate(work):
            log_b, suffix, total = terms[n]
            a = jnp.exp(log_b + suffix)
            if mask is not None:
                a = jnp.where(mask, a, 0.0)
            o = _bdot(vt_ref[j, pair[p], :], a.astype(BF16)) * jnp.exp(carries[p])
            o = jnp.where(lo_half, o[:, :tq], o[:, tq:])
            acc_ref[pair[p], :] = o if first and n < HEAD_PAIRS else acc_ref[pair[p], :] + o
            carries[p] = carries[p] + total
        return tuple(carries)

    carries = key_blocks([qi], (jnp.zeros((1, 2 * tq), F32),) * HEAD_PAIRS, causal, True)
    carries = lax.fori_loop(0, lax.shift_right_logical(qi, 1),
                            lambda i, c: key_blocks([qi - 1 - 2 * i, qi - 2 - 2 * i], c, None, False), carries)
    lax.fori_loop(0, qi & 1, lambda i, c: key_blocks([0], c, None, False), carries)
    o_ref[...] = _rms(acc_ref[...].T, gn_ref[...]).astype(BF16)


def _sb_sample_kernel(pt_ref, bias_ref, q_ref, kn_ref, vn_ref, gn_ref, ck_hbm, cv_hbm, o_ref,
                      kbuf, vbuf, sem, kt_ref, vt_ref, *, n_tok, n_pages, layer):
    b = pl.program_id(0)
    slot = b & 1

    def page_copies(seq, slot):
        copies = []
        for j in range(n_pages):
            page = pt_ref[seq, j]
            copies.append(pltpu.make_async_copy(ck_hbm.at[layer, page], kbuf.at[slot, j], sem.at[slot, 0]))
            copies.append(pltpu.make_async_copy(cv_hbm.at[layer, page], vbuf.at[slot, j], sem.at[slot, 1]))
        return copies

    def start_all(copies):
        for n, c in enumerate(copies):
            c.start(priority=n % 2)

    @pl.when(b == 0)
    def _():
        start_all(page_copies(0, 0))

    @pl.when(b + 1 < pl.num_programs(0))
    def _():
        start_all(page_copies(b + 1, 1 - slot))

    for c in page_copies(b, slot):
        c.wait()

    m = n_tok * N_HEADS
    tk = 2 * PAGE_SIZE
    past = n_pages * PAGE_SIZE
    head_of_lane = lax.broadcasted_iota(jnp.int32, (N_HEADS, D_SB), 1) // HEAD_DIM
    own = head_of_lane == lax.broadcasted_iota(jnp.int32, (N_HEADS, D_SB), 0)
    q = q_ref[...]
    qbd = jnp.concatenate(
        [jnp.where(own, jnp.broadcast_to(q[t:t + 1, :], (N_HEADS, D_SB)), 0.0) for t in range(n_tok)],
        axis=0).astype(BF16)
    bias = bias_ref[:, 0:1]

    for j in range(n_pages):
        kt_ref[:, j * PAGE_SIZE:(j + 1) * PAGE_SIZE] = kbuf[slot, j].astype(BF16)
        vt_ref[:, j * PAGE_SIZE:(j + 1) * PAGE_SIZE] = vbuf[slot, j].astype(BF16)

    pad = jnp.zeros((PAGE_SIZE - SUBLANES, D_SB), F32)
    kn = jnp.concatenate([kn_ref[...], pad], axis=0).astype(BF16)
    vn = jnp.concatenate([vn_ref[...], pad], axis=0).astype(BF16)
    tok = lax.broadcasted_iota(jnp.int32, (m, PAGE_SIZE), 0) // N_HEADS
    mask = lax.broadcasted_iota(jnp.int32, (m, PAGE_SIZE), 1) < tok
    a, carry = _sb_weights(_dot_nt(qbd, kn) + bias, jnp.zeros((m, 1), F32), _suffix_ones(PAGE_SIZE), mask)
    acc = _bdot(a.astype(BF16), vn)

    z = _bdot(qbd, kt_ref[...]) + bias
    ucat = _suffix_ones(tk)
    blocks = [None] * (past // tk)
    for j in reversed(range(past // tk)):
        a, carry = _sb_weights(z[:, j * tk:(j + 1) * tk], carry, ucat, None)
        blocks[j] = a.astype(BF16)
    acc = acc + _dot_nt(jnp.concatenate(blocks, axis=1), vt_ref[...])

    for t in range(n_tok):
        blk = jnp.where(own, acc[t * N_HEADS:(t + 1) * N_HEADS, :], 0.0)
        row = jnp.sum(blk, axis=0, keepdims=True)
        o_ref[t:t + 1, :] = _rms(row, gn_ref[...])


def _mix_in(h_ref, osb_ref, oml_ref, wout_ref, gffn_ref):
    o = jnp.concatenate([osb_ref[...], oml_ref[...]], axis=1)
    h1 = h_ref[...] + _bdot(o, wout_ref[...])
    return h1, _rms(h1, gffn_ref[...]).astype(BF16)


def _ffn_chunk(hn, wup_ref, c):
    cols = slice(c * FF_CHUNK, (c + 1) * FF_CHUNK)
    gate = _bdot(hn, wup_ref[:, cols])
    up = _bdot(hn, wup_ref[:, D_FF + c * FF_CHUNK:D_FF + (c + 1) * FF_CHUNK])
    return cols, gate, up


def _ple_out(h1, act_ref, wdn_ref, p_ref, gple_ref, wg_ref, wp_ref, gfin_ref, out_ref, final):
    h2 = h1 + _bdot(act_ref[...], wdn_ref[...])
    gate = jax.nn.sigmoid(_bdot(_rms(h2, gple_ref[...]).astype(BF16), wg_ref[...]))
    h3 = h2 + gate * _bdot(p_ref[...].astype(BF16), wp_ref[...])
    out_ref[...] = _rms(h3, gfin_ref[...]) if final else h3


def _ffn_prompt_kernel(h_ref, osb_ref, oml_ref, p_ref, wout_ref, gffn_ref, wup_ref, cw_ref, cb_ref,
                       wdn_ref, gple_ref, wg_ref, wp_ref, gfin_ref,
                       out_ref, cst_ref, act_ref, carry_ref, *, tiles_per_seq, final):
    tm = h_ref.shape[0]
    h1, hn = _mix_in(h_ref, osb_ref, oml_ref, wout_ref, gffn_ref)

    @pl.when(pl.program_id(0) % tiles_per_seq == 0)
    def _():
        carry_ref[...] = jnp.zeros_like(carry_ref)

    rowi = lax.broadcasted_iota(jnp.int32, (SUBLANES, FF_CHUNK), 0)
    for c in range(N_FF_CHUNKS):
        cols, gate, up = _ffn_chunk(hn, wup_ref, c)
        prev = carry_ref[:, cols]
        carry_ref[:, cols] = gate[tm - SUBLANES:tm, :]
        cst_ref[:, cols] = gate[tm - (CONV_W - 1):tm, :]
        conv = cb_ref[:, cols] + cw_ref[CONV_W - 1:CONV_W, cols] * gate
        for d in range(1, CONV_W):
            rolled = pltpu.roll(gate, d, axis=0)
            top = jnp.where(rowi < d, pltpu.roll(prev, d, axis=0), rolled[0:SUBLANES])
            shifted = jnp.concatenate([top, rolled[SUBLANES:]], axis=0)
            conv = conv + cw_ref[CONV_W - 1 - d:CONV_W - d, cols] * shifted
        act_ref[:, cols] = (_gelu(conv) * up).astype(BF16)
    _ple_out(h1, act_ref, wdn_ref, p_ref, gple_ref, wg_ref, wp_ref, gfin_ref, out_ref, final)


def _ffn_sample_kernel(h_ref, osb_ref, oml_ref, p_ref, pre_ref, wout_ref, gffn_ref, wup_ref, cw_ref, cb_ref,
                       wdn_ref, gple_ref, wg_ref, wp_ref, gfin_ref,
                       out_ref, cst_ref, act_ref, *, n_tok, n_seq, final):
    h1, hn = _mix_in(h_ref, osb_ref, oml_ref, wout_ref, gffn_ref)
    n_pre = CONV_W - 1
    for c in range(N_FF_CHUNKS):
        cols, gate, up = _ffn_chunk(hn, wup_ref, c)
        gp = [pre_ref[j * n_seq:(j + 1) * n_seq, cols] for j in range(n_pre)]
        gp += [gate[t * n_seq:(t + 1) * n_seq] for t in range(n_tok)]
        for j in range(n_pre):
            cst_ref[j * n_seq:(j + 1) * n_seq, cols] = gp[n_tok + j]
        conv = []
        for t in range(n_tok):
            acc = cb_ref[:, cols]
            for j in range(CONV_W):
                acc = acc + cw_ref[j:j + 1, cols] * gp[t + j]
            conv.append(acc)
        act_ref[:, cols] = (_gelu(jnp.concatenate(conv, axis=0)) * up).astype(BF16)
    _ple_out(h1, act_ref, wdn_ref, p_ref, gple_ref, wg_ref, wp_ref, gfin_ref, out_ref, final)


def _params(*sem):
    return pltpu.CompilerParams(dimension_semantics=sem, vmem_limit_bytes=VMEM_LIMIT)


def _proj_prompt(h, g, win, wkvt, ws_pair, bias_full, gnm, seq, prev, layer):
    n = h.shape[0]
    tps = seq // TM
    n_prev = 0 if prev is None else prev[0].shape[0]
    tile = lambda w: pl.BlockSpec((TM, w), lambda i: (i, 0))
    stack = lambda layers: pl.BlockSpec((layers, None, D_SB, TM), lambda i: (0, i // tps, 0, i % tps))
    kvt_shape = jax.ShapeDtypeStruct((n_prev + 1, n // seq, D_SB, seq), F32)
    return pl.pallas_call(
        functools.partial(_proj_prompt_kernel, n_prev=n_prev),
        grid=(n // TM,),
        in_specs=[tile(D_MODEL), _of_layer((1, D_MODEL), layer), _of_layer((D_MODEL, D_IN), layer),
                  _of_layer((2 * D_SB, D_MODEL), layer), _resident(ws_pair.shape), _resident(bias_full.shape),
                  _of_layer((1, D_MLP), layer)] + [stack(n_prev)] * (2 if n_prev else 0),
        out_specs=[tile(D_SB), tile(D_SB), pl.BlockSpec((TM // TQ, D_SB, TQ), lambda i: (i, 0, 0)),
                   tile(D_MLP), stack(n_prev + 1), stack(n_prev + 1)],
        out_shape=[jax.ShapeDtypeStruct((n, D_SB), BF16), jax.ShapeDtypeStruct((n, D_SB), BF16),
                   jax.ShapeDtypeStruct((n // TQ, D_SB, TQ), BF16), jax.ShapeDtypeStruct((n, D_MLP), BF16),
                   kvt_shape, kvt_shape],
        compiler_params=_params("arbitrary"),
        name="proj_prompt",
    )(h, g, win, wkvt, ws_pair, bias_full, gnm, *(prev or ()))


def _proj_sample(h, g, win, wrow, brow, gnm, n_tok, n_seq, layer):
    n = h.shape[0]
    full = lambda w: pl.BlockSpec((n, w), lambda i: (0, 0))
    return pl.pallas_call(
        functools.partial(_proj_sample_kernel, n_tok=n_tok, n_seq=n_seq),
        grid=(1,),
        in_specs=[full(D_MODEL), _of_layer((1, D_MODEL), layer), _of_layer((D_MODEL, D_IN), layer),
                  _resident(wrow.shape), _resident(brow.shape), _of_layer((1, D_MLP), layer)],
        out_specs=[full(D_SB)] * 5,
        out_shape=[jax.ShapeDtypeStruct((n, D_SB), F32)] * 4 + [jax.ShapeDtypeStruct((n, D_MLP), BF16)],
        compiler_params=_params("arbitrary"),
        name="proj_sample",
    )(h, g, win, wrow, brow, gnm)


def _sb_prompt(bias, q, kb, vtb, gn, seq, layer):
    n = q.shape[0]
    nq = seq // TQ
    return pl.pallas_call(
        functools.partial(_sb_prompt_kernel, layer=layer),
        grid=(n // seq, nq),
        in_specs=[pl.BlockSpec(memory_space=pltpu.SMEM),
                  pl.BlockSpec((TQ, D_SB), lambda b, i: (b * nq + i, 0)),
                  pl.BlockSpec((seq, D_SB), lambda b, i: (b, 0)),
                  pl.BlockSpec((nq, D_SB, TQ), lambda b, i: (b, 0, 0)),
                  _of_layer((1, D_SB), layer)],
        out_specs=pl.BlockSpec((TQ, D_SB), lambda b, i: (b * nq + i, 0)),
        out_shape=jax.ShapeDtypeStruct((n, D_SB), BF16),
        scratch_shapes=[pltpu.VMEM((HEAD_PAIRS, LANES, 2 * TQ), BF16), pltpu.VMEM((D_SB, TQ), F32)],
        compiler_params=_params("arbitrary", "arbitrary"),
        name="sb_prompt",
    )(bias, q, kb, vtb, gn)


def _sb_sample(page_table, bias_rows, q, k_new, v_new, gn, cache_kt, cache_vt, layer):
    n_seq, n_pages = page_table.shape
    n_tok = q.shape[1]
    per_seq = lambda rows: pl.BlockSpec((None, rows, D_SB), lambda b, pt: (b, 0, 0))
    const = lambda shape: pl.BlockSpec(shape, lambda b, pt: (0,) * len(shape), pipeline_mode=pl.Buffered(1))
    in_hbm = pl.BlockSpec(memory_space=pl.ANY)
    page_slots = pltpu.VMEM((2, n_pages, D_SB, PAGE_SIZE), F32)
    grid_spec = pltpu.PrefetchScalarGridSpec(
        num_scalar_prefetch=1,
        grid=(n_seq,),
        in_specs=[const(bias_rows.shape), per_seq(n_tok), per_seq(SUBLANES), per_seq(SUBLANES),
                  _of_layer((1, D_SB), layer), in_hbm, in_hbm],
        out_specs=per_seq(n_tok),
        scratch_shapes=[page_slots, page_slots, pltpu.SemaphoreType.DMA((2, 2)),
                        pltpu.VMEM((D_SB, n_pages * PAGE_SIZE), BF16),
                        pltpu.VMEM((D_SB, n_pages * PAGE_SIZE), BF16)],
    )
    return pl.pallas_call(
        functools.partial(_sb_sample_kernel, n_tok=n_tok, n_pages=n_pages, layer=layer),
        grid_spec=grid_spec,
        out_shape=jax.ShapeDtypeStruct((n_seq, n_tok, D_SB), F32),
        compiler_params=_params("arbitrary"),
        name="sb_sample",
    )(page_table, bias_rows, q, k_new, v_new, gn, cache_kt, cache_vt)


def _weight_specs(layer):
    shapes = [(D_MODEL, D_MODEL), (1, D_MODEL), (D_MODEL, 2 * D_FF), (CONV_W, D_FF), (1, D_FF), (D_FF, D_MODEL),
              (1, D_MODEL), (D_MODEL, D_MODEL), (PLE_DIM, D_MODEL)]
    return [_of_layer(s, layer) for s in shapes] + [_resident((1, D_MODEL))]


def _ffn_prompt(h, osb, oml, p, weights, seq, layer, final):
    n = h.shape[0]
    tile = lambda w: pl.BlockSpec((TM, w), lambda i: (i, 0))
    tiles_per_seq = seq // TM
    return pl.pallas_call(
        functools.partial(_ffn_prompt_kernel, tiles_per_seq=tiles_per_seq, final=final),
        grid=(n // TM,),
        in_specs=[tile(D_MODEL), tile(D_SB), tile(D_MLP),
                  pl.BlockSpec((None, TM, PLE_DIM), lambda i: (layer, i, 0))] + _weight_specs(layer),
        out_specs=[tile(D_MODEL),
                   pl.BlockSpec((None, CONV_W - 1, D_FF), lambda i: (i // tiles_per_seq, 0, 0))],
        out_shape=[jax.ShapeDtypeStruct((n, D_MODEL), F32),
                   jax.ShapeDtypeStruct((n // seq, CONV_W - 1, D_FF), F32)],
        scratch_shapes=[pltpu.VMEM((TM, D_FF), BF16), pltpu.VMEM((SUBLANES, D_FF), F32)],
        compiler_params=_params("arbitrary"),
        name="ffn_prompt",
    )(h, osb, oml, p, *weights)


def _ffn_sample(h, osb, oml, p, prefix, weights, n_tok, n_seq, layer, final):
    n = h.shape[0]
    full = lambda rows, w: pl.BlockSpec((rows, w), lambda i: (0, 0))
    n_pre = (CONV_W - 1) * n_seq
    return pl.pallas_call(
        functools.partial(_ffn_sample_kernel, n_tok=n_tok, n_seq=n_seq, final=final),
        grid=(1,),
        in_specs=[full(n, D_MODEL), full(n, D_SB), full(n, D_MLP), full(n, PLE_DIM),
                  full(n_pre, D_FF)] + _weight_specs(layer),
        out_specs=[full(n, D_MODEL), full(n_pre, D_FF)],
        out_shape=[jax.ShapeDtypeStruct((n, D_MODEL), F32), jax.ShapeDtypeStruct((n_pre, D_FF), F32)],
        scratch_shapes=[pltpu.VMEM((n, D_FF), BF16)],
        compiler_params=_params("arbitrary"),
        name="ffn_sample",
    )(h, osb, oml, p, prefix, *weights)


def kernel(x_prompt, x_sample, cache_k, cache_v, state_conv, page_table, p_prompt, p_sample, norm_mix_g, w_in, sb_bias, gn_sb, gn_mlp, w_spatial, b_spatial, w_out, norm_ffn_g, w_up, conv_w, conv_b, w_down, norm_ple_g, w_ple_gate, w_ple_proj, final_norm_g):
    depth = w_in.shape[0]
    bp, seq, _ = x_prompt.shape
    n_seq, n_tok, _ = x_sample.shape
    n_phys = cache_k.shape[1]

    to_tok_major = lambda x: jnp.swapaxes(x, 0, 1).reshape((x.shape[0] * x.shape[1],) + x.shape[2:])
    to_seq_major = lambda x, t: jnp.swapaxes(x.reshape((t, n_seq) + x.shape[1:]), 0, 1)
    row = lambda x: x.reshape(1, -1)

    hp = x_prompt.reshape(bp * seq, D_MODEL)
    hs = to_tok_major(x_sample)
    page_t = lambda c: jnp.transpose(c, (0, 1, 3, 4, 2)).reshape(depth, n_phys, D_SB, PAGE_SIZE)
    ckt, cvt = page_t(cache_k), page_t(cache_v)
    rows = lambda x: x.reshape(depth, 1, -1)
    win = w_in.astype(BF16)
    wkvt = jnp.swapaxes(w_in[:, :, D_SB:3 * D_SB], 1, 2).astype(BF16)
    weights = (w_out.astype(BF16), rows(norm_ffn_g), w_up.astype(BF16), conv_w, rows(conv_b),
               w_down.astype(BF16), rows(norm_ple_g), w_ple_gate.astype(BF16), w_ple_proj.astype(BF16),
               row(final_norm_g))
    gmix, gnm, gns = rows(norm_mix_g), rows(gn_mlp), rows(gn_sb)
    pp = p_prompt.reshape(depth, bp * seq, PLE_DIM)

    ks_l, vs_l, cp_l, cs_l, ms_l = [], [], [], [], []
    kvt = None
    for i in range(depth):
        final = i == depth - 1

        ws = w_spatial[i]
        ws_pair = jnp.concatenate([ws[0::2], ws[1::2]], axis=2)
        bias_full = jnp.repeat(b_spatial[i].T, GROUP_DIM, axis=1)
        q, kb, vtb, oml, kt_all, vt_all = _proj_prompt(hp, gmix, win, wkvt, ws_pair, bias_full, gnm, seq, kvt, i)
        kvt = (kt_all, vt_all)
        osb = _sb_prompt(sb_bias, q, kb, vtb, gns, seq, i)
        hp, conv_p = _ffn_prompt(hp, osb, oml, pp, weights, seq, i, final)
        cp_l.append(conv_p)

        wrow = jnp.repeat(ws[:, :n_tok, :n_tok].transpose(1, 2, 0).reshape(n_tok * n_tok, N_GROUPS),
                          GROUP_DIM, axis=1)
        brow = jnp.repeat(b_spatial[i][:, :n_tok].T, GROUP_DIM, axis=1)
        q, k, v, vm, oml = _proj_sample(hs, gmix, win, wrow, brow, gnm, n_tok, n_seq, i)
        pad_rows = lambda x: jnp.pad(to_seq_major(x, n_tok), ((0, 0), (0, SUBLANES - n_tok), (0, 0)))
        bias_rows = jnp.broadcast_to(jnp.tile(sb_bias[i], n_tok)[:, None], (n_tok * N_HEADS, LANES))
        osb = _sb_sample(page_table, bias_rows, to_seq_major(q, n_tok), pad_rows(k), pad_rows(v), gns,
                         ckt, cvt, i)
        osb = to_tok_major(osb).astype(BF16)
        hs, conv_s = _ffn_sample(hs, osb, oml, to_tok_major(p_sample[i]), to_tok_major(state_conv[i]),
                                 weights, n_tok, n_seq, i, final)
        ks_l.append(to_seq_major(k, n_tok)); vs_l.append(to_seq_major(v, n_tok))
        cs_l.append(to_seq_major(conv_s, CONV_W - 1)); ms_l.append(to_seq_major(vm, n_tok))

    heads = lambda x: x.reshape(x.shape[:-1] + (N_HEADS, HEAD_DIM))
    from_t = lambda x: jnp.transpose(x.reshape(depth, bp, N_HEADS, HEAD_DIM, seq), (0, 1, 4, 2, 3))
    return (hp.reshape(bp, seq, D_MODEL),
            to_seq_major(hs, n_tok),
            from_t(kvt[0]),
            from_t(kvt[1]),
            heads(jnp.stack(ks_l)),
            heads(jnp.stack(vs_l)),
            jnp.stack(cp_l),
            jnp.stack(cs_l),
            jnp.stack(ms_l))
```

```python
import functools
import math

import jax
import jax.numpy as jnp
from jax import lax
from jax.experimental import pallas as pl
from jax.experimental.pallas import tpu as pltpu

F32 = jnp.float32
BF16 = jnp.bfloat16

D_MODEL = 1024
N_HEADS = 8
HEAD_DIM = 64
D_SB = N_HEADS * HEAD_DIM
N_GROUPS = 8
GROUP_DIM = 64
D_MLP = N_GROUPS * GROUP_DIM
D_IN = 3 * D_SB + 2 * D_MLP
CHUNK = 128
D_FF = 2816
CONV_W = 3
PLE_DIM = 256
PAGE_SIZE = 128
EPS = 1e-6

LANES = 128
SUBLANES = 8
HEAD_PAIRS = D_SB // LANES
FF_CHUNK = 256
N_FF_CHUNKS = D_FF // FF_CHUNK
TM = 512
TQ = 256
VMEM_LIMIT = 56 * 1024 * 1024


def _bdot(a, b):
    return jnp.dot(a, b, preferred_element_type=F32)


def _rms(x, g):
    ms = jnp.mean(x * x, axis=-1, keepdims=True)
    return x * lax.rsqrt(ms + EPS) * g


def _gelu(x):
    c = math.sqrt(2.0 / math.pi)
    return x * (0.5 * (1.0 + jnp.tanh(c * (x + 0.044715 * (x * x * x)))))


def _resident(shape):
    nd = len(shape)
    return pl.BlockSpec(shape, lambda *_: (0,) * nd, pipeline_mode=pl.Buffered(1))


def _of_layer(shape, layer):
    nd = len(shape)
    return pl.BlockSpec((None,) + tuple(shape), lambda *_: (layer,) + (0,) * nd, pipeline_mode=pl.Buffered(1))


def _project(h_ref, g_ref, win_ref):
    hn = _rms(h_ref[...], g_ref[...]).astype(BF16)
    seg = lambda j: _bdot(hn, win_ref[:, j * D_SB:(j + 1) * D_SB])
    return seg(0) * (HEAD_DIM ** -0.5), seg(1), seg(2), _gelu(seg(3)), _gelu(seg(4))


def _proj_prompt_kernel(h_ref, g_ref, win_ref, wkvt_ref, ws_ref, bs_ref, gnm_ref, *refs, n_prev):
    prev = refs[:2] if n_prev else None
    q_ref, kb_ref, vtb_ref, om_ref, kt_ref, vt_ref = refs[2 if n_prev else 0:]
    hn = _rms(h_ref[...], g_ref[...]).astype(BF16)
    seg = lambda j: _bdot(hn, win_ref[:, j * D_SB:(j + 1) * D_SB])
    q_ref[...] = (seg(0) * (HEAD_DIM ** -0.5)).astype(BF16)
    u, vm = _gelu(seg(3)), _gelu(seg(4))
    kt = _dot_nt(wkvt_ref[0:D_SB, :], hn)
    kb_ref[...] = kt.T.astype(BF16)
    vt = _dot_nt(wkvt_ref[D_SB:2 * D_SB, :], hn)
    if n_prev:
        kt_ref[0:n_prev] = prev[0][...]
        vt_ref[0:n_prev] = prev[1][...]
    kt_ref[n_prev] = kt
    vt_ref[n_prev] = vt
    for c in range(vtb_ref.shape[0]):
        vtb_ref[c] = vt[:, c * TQ:(c + 1) * TQ].astype(BF16)

    row = lax.broadcasted_iota(jnp.int32, (CHUNK, 2 * CHUNK), 0)
    col = lax.broadcasted_iota(jnp.int32, (CHUNK, 2 * CHUNK), 1)
    tril = (col & (CHUNK - 1)) <= row
    wpair = [jnp.where(tril, ws_ref[p], 0.0).astype(BF16) for p in range(HEAD_PAIRS)]
    lo_half = lax.broadcasted_iota(jnp.int32, (CHUNK, LANES), 1) < GROUP_DIM
    for c in range(h_ref.shape[0] // CHUNK):
        rows = slice(c * CHUNK, (c + 1) * CHUNK)
        parts = []
        for p in range(HEAD_PAIRS):
            s = vm[rows, p * LANES:(p + 1) * LANES]
            rhs = jnp.concatenate([jnp.where(lo_half, s, 0.0).astype(BF16),
                                   jnp.where(lo_half, 0.0, s).astype(BF16)], axis=0)
            parts.append(_bdot(wpair[p], rhs))
        mixed = jnp.concatenate(parts, axis=1) + bs_ref[...]
        om_ref[rows, :] = _rms(u[rows] * mixed, gnm_ref[...]).astype(BF16)


def _proj_sample_kernel(h_ref, g_ref, win_ref, wrow_ref, brow_ref, gnm_ref,
                        q_ref, k_ref, v_ref, vm_ref, om_ref, *, n_tok, n_seq):
    q, k, v, u, vm = _project(h_ref, g_ref, win_ref)
    q_ref[...] = q
    k_ref[...] = k
    v_ref[...] = v
    vm_ref[...] = vm
    for t in range(n_tok):
        mixed = brow_ref[t:t + 1, :]
        for s in range(t + 1):
            w = wrow_ref[t * n_tok + s:t * n_tok + s + 1, :]
            mixed = mixed + w * vm[s * n_seq:(s + 1) * n_seq]
        rows = slice(t * n_seq, (t + 1) * n_seq)
        om_ref[rows, :] = _rms(u[rows] * mixed, gnm_ref[...]).astype(BF16)


def _dot_nt(a, b):
    return lax.dot_general(a, b, (((1,), (1,)), ((), ())), preferred_element_type=F32)


def _sb_suffix(z, ucat, mask, key_axis=1):
    lg = jnp.log(1.0 + jnp.exp(-jnp.abs(z)))
    log_b = jnp.minimum(z, 0.0) - lg
    log_1mb = log_b - z
    if mask is not None:
        log_1mb = jnp.where(mask, log_1mb, 0.0)
    hi = log_1mb.astype(BF16)
    lo = (log_1mb - hi.astype(F32)).astype(BF16)
    split = jnp.concatenate([hi, lo], axis=key_axis)
    suffix = _bdot(split, ucat) if key_axis == 1 else _bdot(ucat, split)
    first = (slice(None), slice(0, 1)) if key_axis == 1 else (slice(0, 1), slice(None))
    return log_b, suffix, suffix[first] + log_1mb[first]


def _sb_weights(z, carry, ucat, mask):
    log_b, suffix, total = _sb_suffix(z, ucat, mask)
    a = jnp.exp(log_b + suffix + carry)
    if mask is not None:
        a = jnp.where(mask, a, 0.0)
    return a, carry + total


def _suffix_ones(tk, key_axis=1):
    r = lax.broadcasted_iota(jnp.int32, (tk, tk), 0)
    c = lax.broadcasted_iota(jnp.int32, (tk, tk), 1)
    u = jnp.where(r > c if key_axis == 1 else c > r, 1.0, 0.0).astype(BF16)
    return jnp.concatenate([u, u], axis=1 - key_axis)


def _sb_prompt_kernel(bias_ref, q_ref, k_ref, vt_ref, gn_ref, o_ref, qmt_ref, acc_ref, *, layer):
    tq = q_ref.shape[0]
    qi = pl.program_id(1)
    ucat = _suffix_ones(tq, key_axis=0)
    r = lax.broadcasted_iota(jnp.int32, (tq, 2 * tq), 0)
    c = lax.broadcasted_iota(jnp.int32, (tq, 2 * tq), 1)
    causal = r < (c & (tq - 1))
    lo_half = lax.broadcasted_iota(jnp.int32, (LANES, tq), 0) < HEAD_DIM
    pair = [slice(p * LANES, (p + 1) * LANES) for p in range(HEAD_PAIRS)]

    for p in range(HEAD_PAIRS):
        q2t = q_ref[:, pair[p]].astype(F32).T
        qmt_ref[p] = jnp.concatenate([jnp.where(lo_half, q2t, 0.0), jnp.where(lo_half, 0.0, q2t)],
                                     axis=1).astype(BF16)

    def key_blocks(js, carries, mask, first):
        work = [(j, p) for j in js for p in range(HEAD_PAIRS)]
        logits, terms = [], []
        for j, p in work:
            start = pl.multiple_of(j * tq, tq)
            s = _bdot(k_ref[pl.ds(start, tq), pair[p]], qmt_ref[p])
            logits.append(jnp.concatenate([s[:, :tq] + bias_ref[layer, 2 * p],
                                           s[:, tq:] + bias_ref[layer, 2 * p + 1]], axis=1))
        for z in logits:
            terms.append(_sb_suffix(z, ucat, mask, key_axis=0))
        carries = list(carries)
        for n, (j, p) in enumerate(work):
            log_b, suffix, total = terms[n]
            a = jnp.exp(log_b + suffix)
            if mask is not None:
                a = jnp.where(mask, a, 0.0)
            o = _bdot(vt_ref[j, pair[p], :], a.astype(BF16)) * jnp.exp(carries[p])
            o = jnp.where(lo_half, o[:, :tq], o[:, tq:])
            acc_ref[pair[p], :] = o if first and n < HEAD_PAIRS else acc_ref[pair[p], :] + o
            carries[p] = carries[p] + total
        return tuple(carries)

    carries = key_blocks([qi], (jnp.zeros((1, 2 * tq), F32),) * HEAD_PAIRS, causal, True)
    carries = lax.fori_loop(0, lax.shift_right_logical(qi, 1),
                            lambda i, c: key_blocks([qi - 1 - 2 * i, qi - 2 - 2 * i], c, None, False), carries)
    lax.fori_loop(0, qi & 1, lambda i, c: key_blocks([0], c, None, False), carries)
    o_ref[...] = _rms(acc_ref[...].T, gn_ref[...]).astype(BF16)


def _sb_sample_kernel(pt_ref, bias_ref, q_ref, kn_ref, vn_ref, gn_ref, ck_hbm, cv_hbm, o_ref,
                      kbuf, vbuf, sem, kt_ref, vt_ref, *, n_tok, n_pages, layer):
    b = pl.program_id(0)
    slot = b & 1

    def page_copies(seq, slot):
        copies = []
        for j in range(n_pages):
            page = pt_ref[seq, j]
            copies.append(pltpu.make_async_copy(ck_hbm.at[layer, page], kbuf.at[slot, j], sem.at[slot, 0]))
            copies.append(pltpu.make_async_copy(cv_hbm.at[layer, page], vbuf.at[slot, j], sem.at[slot, 1]))
        return copies

    @pl.when(b == 0)
    def _():
        for c in page_copies(0, 0):
            c.start()

    @pl.when(b + 1 < pl.num_programs(0))
    def _():
        for c in page_copies(b + 1, 1 - slot):
            c.start()

    for c in page_copies(b, slot):
        c.wait()

    m = n_tok * N_HEADS
    tk = 2 * PAGE_SIZE
    past = n_pages * PAGE_SIZE
    head_of_lane = lax.broadcasted_iota(jnp.int32, (N_HEADS, D_SB), 1) // HEAD_DIM
    own = head_of_lane == lax.broadcasted_iota(jnp.int32, (N_HEADS, D_SB), 0)
    q = q_ref[...]
    qbd = jnp.concatenate(
        [jnp.where(own, jnp.broadcast_to(q[t:t + 1, :], (N_HEADS, D_SB)), 0.0) for t in range(n_tok)],
        axis=0).astype(BF16)
    bias = bias_ref[:, 0:1]

    for j in range(n_pages):
        kt_ref[:, j * PAGE_SIZE:(j + 1) * PAGE_SIZE] = kbuf[slot, j].astype(BF16)
        vt_ref[:, j * PAGE_SIZE:(j + 1) * PAGE_SIZE] = vbuf[slot, j].astype(BF16)

    pad = jnp.zeros((PAGE_SIZE - SUBLANES, D_SB), F32)
    kn = jnp.concatenate([kn_ref[...], pad], axis=0).astype(BF16)
    vn = jnp.concatenate([vn_ref[...], pad], axis=0).astype(BF16)
    tok = lax.broadcasted_iota(jnp.int32, (m, PAGE_SIZE), 0) // N_HEADS
    mask = lax.broadcasted_iota(jnp.int32, (m, PAGE_SIZE), 1) < tok
    a, carry = _sb_weights(_dot_nt(qbd, kn) + bias, jnp.zeros((m, 1), F32), _suffix_ones(PAGE_SIZE), mask)
    acc = _bdot(a.astype(BF16), vn)

    z = _bdot(qbd, kt_ref[...]) + bias
    ucat = _suffix_ones(tk)
    blocks = [None] * (past // tk)
    for j in reversed(range(past // tk)):
        a, carry = _sb_weights(z[:, j * tk:(j + 1) * tk], carry, ucat, None)
        blocks[j] = a.astype(BF16)
    acc = acc + _dot_nt(jnp.concatenate(blocks, axis=1), vt_ref[...])

    for t in range(n_tok):
        blk = jnp.where(own, acc[t * N_HEADS:(t + 1) * N_HEADS, :], 0.0)
        row = jnp.sum(blk, axis=0, keepdims=True)
        o_ref[t:t + 1, :] = _rms(row, gn_ref[...])


def _mix_in(h_ref, osb_ref, oml_ref, wout_ref, gffn_ref):
    o = jnp.concatenate([osb_ref[...], oml_ref[...]], axis=1)
    h1 = h_ref[...] + _bdot(o, wout_ref[...])
    return h1, _rms(h1, gffn_ref[...]).astype(BF16)


def _ffn_chunk(hn, wup_ref, c):
    cols = slice(c * FF_CHUNK, (c + 1) * FF_CHUNK)
    gate = _bdot(hn, wup_ref[:, cols])
    up = _bdot(hn, wup_ref[:, D_FF + c * FF_CHUNK:D_FF + (c + 1) * FF_CHUNK])
    return cols, gate, up


def _ple_out(h1, act_ref, wdn_ref, p_ref, gple_ref, wg_ref, wp_ref, gfin_ref, out_ref, final):
    h2 = h1 + _bdot(act_ref[...], wdn_ref[...])
    gate = jax.nn.sigmoid(_bdot(_rms(h2, gple_ref[...]).astype(BF16), wg_ref[...]))
    h3 = h2 + gate * _bdot(p_ref[...].astype(BF16), wp_ref[...])
    out_ref[...] = _rms(h3, gfin_ref[...]) if final else h3


def _ffn_prompt_kernel(h_ref, osb_ref, oml_ref, p_ref, wout_ref, gffn_ref, wup_ref, cw_ref, cb_ref,
                       wdn_ref, gple_ref, wg_ref, wp_ref, gfin_ref,
                       out_ref, cst_ref, act_ref, carry_ref, *, tiles_per_seq, final):
    tm = h_ref.shape[0]
    h1, hn = _mix_in(h_ref, osb_ref, oml_ref, wout_ref, gffn_ref)

    @pl.when(pl.program_id(0) % tiles_per_seq == 0)
    def _():
        carry_ref[...] = jnp.zeros_like(carry_ref)

    rowi = lax.broadcasted_iota(jnp.int32, (SUBLANES, FF_CHUNK), 0)
    for c in range(N_FF_CHUNKS):
        cols, gate, up = _ffn_chunk(hn, wup_ref, c)
        prev = carry_ref[:, cols]
        carry_ref[:, cols] = gate[tm - SUBLANES:tm, :]
        cst_ref[:, cols] = gate[tm - (CONV_W - 1):tm, :]
        conv = cb_ref[:, cols] + cw_ref[CONV_W - 1:CONV_W, cols] * gate
        for d in range(1, CONV_W):
            rolled = pltpu.roll(gate, d, axis=0)
            top = jnp.where(rowi < d, pltpu.roll(prev, d, axis=0), rolled[0:SUBLANES])
            shifted = jnp.concatenate([top, rolled[SUBLANES:]], axis=0)
            conv = conv + cw_ref[CONV_W - 1 - d:CONV_W - d, cols] * shifted
        act_ref[:, cols] = (_gelu(conv) * up).astype(BF16)
    _ple_out(h1, act_ref, wdn_ref, p_ref, gple_ref, wg_ref, wp_ref, gfin_ref, out_ref, final)


def _ffn_sample_kernel(h_ref, osb_ref, oml_ref, p_ref, pre_ref, wout_ref, gffn_ref, wup_ref, cw_ref, cb_ref,
                       wdn_ref, gple_ref, wg_ref, wp_ref, gfin_ref,
                       out_ref, cst_ref, act_ref, *, n_tok, n_seq, final):
    h1, hn = _mix_in(h_ref, osb_ref, oml_ref, wout_ref, gffn_ref)
    n_pre = CONV_W - 1
    for c in range(N_FF_CHUNKS):
        cols, gate, up = _ffn_chunk(hn, wup_ref, c)
        gp = [pre_ref[j * n_seq:(j + 1) * n_seq, cols] for j in range(n_pre)]
        gp += [gate[t * n_seq:(t + 1) * n_seq] for t in range(n_tok)]
        for j in range(n_pre):
            cst_ref[j * n_seq:(j + 1) * n_seq, cols] = gp[n_tok + j]
        conv = []
        for t in range(n_tok):
            acc = cb_ref[:, cols]
            for j in range(CONV_W):
                acc = acc + cw_ref[j:j + 1, cols] * gp[t + j]
            conv.append(acc)
        act_ref[:, cols] = (_gelu(jnp.concatenate(conv, axis=0)) * up).astype(BF16)
    _ple_out(h1, act_ref, wdn_ref, p_ref, gple_ref, wg_ref, wp_ref, gfin_ref, out_ref, final)


def _params(*sem):
    return pltpu.CompilerParams(dimension_semantics=sem, vmem_limit_bytes=VMEM_LIMIT)


def _proj_prompt(h, g, win, wkvt, ws_pair, bias_full, gnm, seq, prev, layer):
    n = h.shape[0]
    tps = seq // TM
    n_prev = 0 if prev is None else prev[0].shape[0]
    tile = lambda w: pl.BlockSpec((TM, w), lambda i: (i, 0))
    stack = lambda layers: pl.BlockSpec((layers, None, D_SB, TM), lambda i: (0, i // tps, 0, i % tps))
    kvt_shape = jax.ShapeDtypeStruct((n_prev + 1, n // seq, D_SB, seq), F32)
    return pl.pallas_call(
        functools.partial(_proj_prompt_kernel, n_prev=n_prev),
        grid=(n // TM,),
        in_specs=[tile(D_MODEL), _of_layer((1, D_MODEL), layer), _of_layer((D_MODEL, D_IN), layer),
                  _of_layer((2 * D_SB, D_MODEL), layer), _resident(ws_pair.shape), _resident(bias_full.shape),
                  _of_layer((1, D_MLP), layer)] + [stack(n_prev)] * (2 if n_prev else 0),
        out_specs=[tile(D_SB), tile(D_SB), pl.BlockSpec((TM // TQ, D_SB, TQ), lambda i: (i, 0, 0)),
                   tile(D_MLP), stack(n_prev + 1), stack(n_prev + 1)],
        out_shape=[jax.ShapeDtypeStruct((n, D_SB), BF16), jax.ShapeDtypeStruct((n, D_SB), BF16),
                   jax.ShapeDtypeStruct((n // TQ, D_SB, TQ), BF16), jax.ShapeDtypeStruct((n, D_MLP), BF16),
                   kvt_shape, kvt_shape],
        compiler_params=_params("arbitrary"),
        name="proj_prompt",
    )(h, g, win, wkvt, ws_pair, bias_full, gnm, *(prev or ()))


def _proj_sample(h, g, win, wrow, brow, gnm, n_tok, n_seq, layer):
    n = h.shape[0]
    full = lambda w: pl.BlockSpec((n, w), lambda i: (0, 0))
    return pl.pallas_call(
        functools.partial(_proj_sample_kernel, n_tok=n_tok, n_seq=n_seq),
        grid=(1,),
        in_specs=[full(D_MODEL), _of_layer((1, D_MODEL), layer), _of_layer((D_MODEL, D_IN), layer),
                  _resident(wrow.shape), _resident(brow.shape), _of_layer((1, D_MLP), layer)],
        out_specs=[full(D_SB)] * 5,
        out_shape=[jax.ShapeDtypeStruct((n, D_SB), F32)] * 4 + [jax.ShapeDtypeStruct((n, D_MLP), BF16)],
        compiler_params=_params("arbitrary"),
        name="proj_sample",
    )(h, g, win, wrow, brow, gnm)


def _sb_prompt(bias, q, kb, vtb, gn, seq, layer):
    n = q.shape[0]
    nq = seq // TQ
    return pl.pallas_call(
        functools.partial(_sb_prompt_kernel, layer=layer),
        grid=(n // seq, nq),
        in_specs=[pl.BlockSpec(memory_space=pltpu.SMEM),
                  pl.BlockSpec((TQ, D_SB), lambda b, i: (b * nq + i, 0)),
                  pl.BlockSpec((seq, D_SB), lambda b, i: (b, 0)),
                  pl.BlockSpec((nq, D_SB, TQ), lambda b, i: (b, 0, 0)),
                  _of_layer((1, D_SB), layer)],
        out_specs=pl.BlockSpec((TQ, D_SB), lambda b, i: (b * nq + i, 0)),
        out_shape=jax.ShapeDtypeStruct((n, D_SB), BF16),
        scratch_shapes=[pltpu.VMEM((HEAD_PAIRS, LANES, 2 * TQ), BF16), pltpu.VMEM((D_SB, TQ), F32)],
        compiler_params=_params("arbitrary", "arbitrary"),
        name="sb_prompt",
    )(bias, q, kb, vtb, gn)


def _sb_sample(page_table, bias_rows, q, k_new, v_new, gn, cache_kt, cache_vt, layer):
    n_seq, n_pages = page_table.shape
    n_tok = q.shape[1]
    per_seq = lambda rows: pl.BlockSpec((None, rows, D_SB), lambda b, pt: (b, 0, 0))
    const = lambda shape: pl.BlockSpec(shape, lambda b, pt: (0,) * len(shape), pipeline_mode=pl.Buffered(1))
    in_hbm = pl.BlockSpec(memory_space=pl.ANY)
    page_slots = pltpu.VMEM((2, n_pages, D_SB, PAGE_SIZE), F32)
    grid_spec = pltpu.PrefetchScalarGridSpec(
        num_scalar_prefetch=1,
        grid=(n_seq,),
        in_specs=[const(bias_rows.shape), per_seq(n_tok), per_seq(SUBLANES), per_seq(SUBLANES),
                  _of_layer((1, D_SB), layer), in_hbm, in_hbm],
        out_specs=per_seq(n_tok),
        scratch_shapes=[page_slots, page_slots, pltpu.SemaphoreType.DMA((2, 2)),
                        pltpu.VMEM((D_SB, n_pages * PAGE_SIZE), BF16),
                        pltpu.VMEM((D_SB, n_pages * PAGE_SIZE), BF16)],
    )
    return pl.pallas_call(
        functools.partial(_sb_sample_kernel, n_tok=n_tok, n_pages=n_pages, layer=layer),
        grid_spec=grid_spec,
        out_shape=jax.ShapeDtypeStruct((n_seq, n_tok, D_SB), F32),
        compiler_params=_params("arbitrary"),
        name="sb_sample",
    )(page_table, bias_rows, q, k_new, v_new, gn, cache_kt, cache_vt)


def _weight_specs(layer):
    shapes = [(D_MODEL, D_MODEL), (1, D_MODEL), (D_MODEL, 2 * D_FF), (CONV_W, D_FF), (1, D_FF), (D_FF, D_MODEL),
              (1, D_MODEL), (D_MODEL, D_MODEL), (PLE_DIM, D_MODEL)]
    return [_of_layer(s, layer) for s in shapes] + [_resident((1, D_MODEL))]


def _ffn_prompt(h, osb, oml, p, weights, seq, layer, final):
    n = h.shape[0]
    tile = lambda w: pl.BlockSpec((TM, w), lambda i: (i, 0))
    tiles_per_seq = seq // TM
    return pl.pallas_call(
        functools.partial(_ffn_prompt_kernel, tiles_per_seq=tiles_per_seq, final=final),
        grid=(n // TM,),
        in_specs=[tile(D_MODEL), tile(D_SB), tile(D_MLP),
                  pl.BlockSpec((None, TM, PLE_DIM), lambda i: (layer, i, 0))] + _weight_specs(layer),
        out_specs=[tile(D_MODEL),
                   pl.BlockSpec((None, CONV_W - 1, D_FF), lambda i: (i // tiles_per_seq, 0, 0))],
        out_shape=[jax.ShapeDtypeStruct((n, D_MODEL), F32),
                   jax.ShapeDtypeStruct((n // seq, CONV_W - 1, D_FF), F32)],
        scratch_shapes=[pltpu.VMEM((TM, D_FF), BF16), pltpu.VMEM((SUBLANES, D_FF), F32)],
        compiler_params=_params("arbitrary"),
        name="ffn_prompt",
    )(h, osb, oml, p, *weights)


def _ffn_sample(h, osb, oml, p, prefix, weights, n_tok, n_seq, layer, final):
    n = h.shape[0]
    full = lambda rows, w: pl.BlockSpec((rows, w), lambda i: (0, 0))
    n_pre = (CONV_W - 1) * n_seq
    return pl.pallas_call(
        functools.partial(_ffn_sample_kernel, n_tok=n_tok, n_seq=n_seq, final=final),
        grid=(1,),
        in_specs=[full(n, D_MODEL), full(n, D_SB), full(n, D_MLP), full(n, PLE_DIM),
                  full(n_pre, D_FF)] + _weight_specs(layer),
        out_specs=[full(n, D_MODEL), full(n_pre, D_FF)],
        out_shape=[jax.ShapeDtypeStruct((n, D_MODEL), F32), jax.ShapeDtypeStruct((n_pre, D_FF), F32)],
        scratch_shapes=[pltpu.VMEM((n, D_FF), BF16)],
        compiler_params=_params("arbitrary"),
        name="ffn_sample",
    )(h, osb, oml, p, prefix, *weights)


def kernel(x_prompt, x_sample, cache_k, cache_v, state_conv, page_table, p_prompt, p_sample, norm_mix_g, w_in, sb_bias, gn_sb, gn_mlp, w_spatial, b_spatial, w_out, norm_ffn_g, w_up, conv_w, conv_b, w_down, norm_ple_g, w_ple_gate, w_ple_proj, final_norm_g):
    depth = w_in.shape[0]
    bp, seq, _ = x_prompt.shape
    n_seq, n_tok, _ = x_sample.shape
    n_phys = cache_k.shape[1]

    to_tok_major = lambda x: jnp.swapaxes(x, 0, 1).reshape((x.shape[0] * x.shape[1],) + x.shape[2:])
    to_seq_major = lambda x, t: jnp.swapaxes(x.reshape((t, n_seq) + x.shape[1:]), 0, 1)
    row = lambda x: x.reshape(1, -1)

    hp = x_prompt.reshape(bp * seq, D_MODEL)
    hs = to_tok_major(x_sample)
    page_t = lambda c: jnp.transpose(c, (0, 1, 3, 4, 2)).reshape(depth, n_phys, D_SB, PAGE_SIZE)
    ckt, cvt = page_t(cache_k), page_t(cache_v)
    rows = lambda x: x.reshape(depth, 1, -1)
    win = w_in.astype(BF16)
    wkvt = jnp.swapaxes(w_in[:, :, D_SB:3 * D_SB], 1, 2).astype(BF16)
    weights = (w_out.astype(BF16), rows(norm_ffn_g), w_up.astype(BF16), conv_w, rows(conv_b),
               w_down.astype(BF16), rows(norm_ple_g), w_ple_gate.astype(BF16), w_ple_proj.astype(BF16),
               row(final_norm_g))
    gmix, gnm, gns = rows(norm_mix_g), rows(gn_mlp), rows(gn_sb)
    pp = p_prompt.reshape(depth, bp * seq, PLE_DIM)

    ks_l, vs_l, cp_l, cs_l, ms_l = [], [], [], [], []
    kvt = None
    for i in range(depth):
        final = i == depth - 1

        ws = w_spatial[i]
        ws_pair = jnp.concatenate([ws[0::2], ws[1::2]], axis=2)
        bias_full = jnp.repeat(b_spatial[i].T, GROUP_DIM, axis=1)
        q, kb, vtb, oml, kt_all, vt_all = _proj_prompt(hp, gmix, win, wkvt, ws_pair, bias_full, gnm, seq, kvt, i)
        kvt = (kt_all, vt_all)
        osb = _sb_prompt(sb_bias, q, kb, vtb, gns, seq, i)
        hp, conv_p = _ffn_prompt(hp, osb, oml, pp, weights, seq, i, final)
        cp_l.append(conv_p)

        wrow = jnp.repeat(ws[:, :n_tok, :n_tok].transpose(1, 2, 0).reshape(n_tok * n_tok, N_GROUPS),
                          GROUP_DIM, axis=1)
        brow = jnp.repeat(b_spatial[i][:, :n_tok].T, GROUP_DIM, axis=1)
        q, k, v, vm, oml = _proj_sample(hs, gmix, win, wrow, brow, gnm, n_tok, n_seq, i)
        pad_rows = lambda x: jnp.pad(to_seq_major(x, n_tok), ((0, 0), (0, SUBLANES - n_tok), (0, 0)))
        bias_rows = jnp.broadcast_to(jnp.tile(sb_bias[i], n_tok)[:, None], (n_tok * N_HEADS, LANES))
        osb = _sb_sample(page_table, bias_rows, to_seq_major(q, n_tok), pad_rows(k), pad_rows(v), gns,
                         ckt, cvt, i)
        osb = to_tok_major(osb).astype(BF16)
        hs, conv_s = _ffn_sample(hs, osb, oml, to_tok_major(p_sample[i]), to_tok_major(state_conv[i]),
                                 weights, n_tok, n_seq, i, final)
        ks_l.append(to_seq_major(k, n_tok)); vs_l.append(to_seq_major(v, n_tok))
        cs_l.append(to_seq_major(conv_s, CONV_W - 1)); ms_l.append(to_seq_major(vm, n_tok))

    heads = lambda x: x.reshape(x.shape[:-1] + (N_HEADS, HEAD_DIM))
    from_t = lambda x: jnp.transpose(x.reshape(depth, bp, N_HEADS, HEAD_DIM, seq), (0, 1, 4, 2, 3))
    return (hp.reshape(bp, seq, D_MODEL),
            to_seq_major(hs, n_tok),
            from_t(kvt[0]),
            from_t(kvt[1]),
            heads(jnp.stack(ks_l)),
            heads(jnp.stack(vs_l)),
            jnp.stack(cp_l),
            jnp.stack(cs_l),
            jnp.stack(ms_l))
```
